```python
import jax, jax.numpy as jnp
from jax import lax
import numpy as np

D_MODEL = 2048
BATCH = 1
SEQ = 8192
DEPTH = 4

GRID_W = 64
MEM_LEN = 256
EPS = 1e-6

NA_HEADS = 8
NA_HEAD_DIM = 128
NA_WIDTH = NA_HEADS * NA_HEAD_DIM
NA_ROWS_MAX = 8
NA_COLS = 16

MLA_HEADS = 8
MLA_Q_RANK = 512
MLA_KV_RANK = 256
MLA_NOPE_DIM = 128
MLA_ROPE_DIM = 64
MLA_V_DIM = 128
MLA_QK_DIM = MLA_NOPE_DIM + MLA_ROPE_DIM
MLA_WIDTH = MLA_HEADS * MLA_V_DIM
ROPE_THETA = 10000.0
Q_BLOCK = 128

D_MIX = NA_WIDTH + MLA_WIDTH
IN_WIDTH = 3 * NA_WIDTH + MLA_Q_RANK + MLA_KV_RANK + MLA_ROPE_DIM

X_HEADS = 4
X_HEAD_DIM = D_MODEL // X_HEADS

D_FF = ((8 * D_MODEL + 3 * 256 - 1) // (3 * 256)) * 256

kernel_name = "hybrid_natten_mla_encoder"


def rmsnorm(x, g):
    xf = x.astype(jnp.float32)
    y = xf * lax.rsqrt(jnp.mean(xf * xf, axis=-1, keepdims=True) + EPS)
    return (y * g.astype(jnp.float32)).astype(x.dtype)


def rope_tables(seq_len):
    inv = 1.0 / (ROPE_THETA ** (jnp.arange(0, MLA_ROPE_DIM, 2, dtype=jnp.float32) / MLA_ROPE_DIM))
    ang = jnp.arange(seq_len, dtype=jnp.float32)[:, None] * inv[None, :]
    return jnp.cos(ang), jnp.sin(ang)


def apply_rope(x, cos, sin):
    xf = x.astype(jnp.float32)
    x1, x2 = jnp.split(xf, 2, axis=-1)
    return jnp.concatenate([x1 * cos - x2 * sin, x1 * sin + x2 * cos], axis=-1).astype(x.dtype)


def neighbourhood_attention(q, k, v, rpb):
    B, S, H, dh = q.shape
    rows = S // GRID_W
    kr = min(NA_ROWS_MAX, rows)
    qg = q.reshape(B, rows, GRID_W, H, dh)
    kg = k.reshape(B, rows, GRID_W, H, dh)
    vg = v.reshape(B, rows, GRID_W, H, dh)
    row_start = jnp.clip(jnp.arange(rows) - kr // 2, 0, rows - kr)
    col_start = np.clip(np.arange(GRID_W) - NA_COLS // 2, 0, GRID_W - NA_COLS)
    col_idx = col_start[:, None] + np.arange(NA_COLS)[None, :]
    dc = col_idx - np.arange(GRID_W)[:, None] + (NA_COLS - 1)
    bias_c = rpb[:, :, dc]
    scale = dh ** -0.5

    def one_row(r):
        rs = row_start[r]
        kb = lax.dynamic_slice_in_dim(kg, rs, kr, axis=1)[:, :, col_idx]
        vb = lax.dynamic_slice_in_dim(vg, rs, kr, axis=1)[:, :, col_idx]
        qr = lax.dynamic_index_in_dim(qg, r, axis=1, keepdims=False)
        s = jnp.einsum('bqhd,brqkhd->bhqrk', qr, kb).astype(jnp.float32) * scale
        dr = rs + jnp.arange(kr) - r + (NA_ROWS_MAX - 1)
        bias = jnp.transpose(bias_c[:, dr], (0, 2, 1, 3)).astype(jnp.float32)
        s = s + bias[None]
        p = jax.nn.softmax(s.reshape(B, H, GRID_W, kr * NA_COLS), axis=-1)
        p = p.reshape(B, H, GRID_W, kr, NA_COLS).astype(v.dtype)
        return jnp.einsum('bhqrk,brqkhd->bqhd', p, vb)

    out = lax.map(one_row, jnp.arange(rows))
    return jnp.moveaxis(out, 0, 1).reshape(B, S, H * dh)


def latent_attention(c_q, c_kv, k_rope, q_norm, kv_norm, w_uq, w_ukv, cos, sin):
    B, S, _ = c_q.shape
    q = (rmsnorm(c_q, q_norm) @ w_uq).reshape(B, S, MLA_HEADS, MLA_QK_DIM)
    q_nope, q_rope = q[..., :MLA_NOPE_DIM], q[..., MLA_NOPE_DIM:]
    q_rope = apply_rope(q_rope, cos[:, None, :], sin[:, None, :])
    kv = (rmsnorm(c_kv, kv_norm) @ w_ukv).reshape(B, S, MLA_HEADS, MLA_NOPE_DIM + MLA_V_DIM)
    k_nope, v = kv[..., :MLA_NOPE_DIM], kv[..., MLA_NOPE_DIM:]
    k_r = apply_rope(k_rope, cos, sin)
    nb = S // Q_BLOCK
    qn_b = jnp.moveaxis(q_nope.reshape(B, nb, Q_BLOCK, MLA_HEADS, MLA_NOPE_DIM), 1, 0)
    qr_b = jnp.moveaxis(q_rope.reshape(B, nb, Q_BLOCK, MLA_HEADS, MLA_ROPE_DIM), 1, 0)
    scale = MLA_QK_DIM ** -0.5

    def block(args):
        qn, qr = args
        s = (jnp.einsum('bqhd,bkhd->bhqk', qn, k_nope)
             + jnp.einsum('bqhr,bkr->bhqk', qr, k_r)).astype(jnp.float32) * scale
        p = jax.nn.softmax(s, axis=-1).astype(v.dtype)
        return jnp.einsum('bhqk,bkhd->bqhd', p, v)

    out = lax.map(block, (qn_b, qr_b))
    return jnp.moveaxis(out, 0, 1).reshape(B, S, MLA_WIDTH)


def memory_attention(h, mem_n, w_q, w_k, w_v, w_o):
    B, S, _ = h.shape
    M = mem_n.shape[1]
    q = (h @ w_q).reshape(B, S, X_HEADS, X_HEAD_DIM)
    k = (mem_n @ w_k).reshape(B, M, X_HEADS, X_HEAD_DIM)
    v = (mem_n @ w_v).reshape(B, M, X_HEADS, X_HEAD_DIM)
    s = jnp.einsum('bqhd,bkhd->bhqk', q, k).astype(jnp.float32) * (X_HEAD_DIM ** -0.5)
    p = jax.nn.softmax(s, axis=-1).astype(v.dtype)
    o = jnp.einsum('bhqk,bkhd->bqhd', p, v).reshape(B, S, D_MODEL)
    return o @ w_o


def swiglu(h, w_gate, w_up, w_down):
    return (jax.nn.silu(h @ w_gate) * (h @ w_up)) @ w_down


def setup_inputs(seed: int = 0) -> dict:
    key = jax.random.key(seed)
    ks = jax.random.split(key, 24)

    def dense(k, shape, fan_in):
        return jax.random.normal(k, shape, jnp.float32) * (fan_in ** -0.5)

    def gain(k, shape):
        return 1.0 + 0.02 * jax.random.normal(k, shape, jnp.float32)

    L = DEPTH
    return {
        "x": jax.random.normal(ks[0], (BATCH, SEQ, D_MODEL), jnp.float32),
        "mem": jax.random.normal(ks[1], (BATCH, MEM_LEN, D_MODEL), jnp.float32),
        "ln_mix": gain(ks[2], (L, D_MODEL)),
        "w_in": dense(ks[3], (L, D_MODEL, IN_WIDTH), D_MODEL),
        "q_norm": gain(ks[4], (L, MLA_Q_RANK)),
        "kv_norm": gain(ks[5], (L, MLA_KV_RANK)),
        "w_uq": dense(ks[6], (L, MLA_Q_RANK, MLA_HEADS * MLA_QK_DIM), MLA_Q_RANK),
        "w_ukv": dense(ks[7], (L, MLA_KV_RANK, MLA_HEADS * (MLA_NOPE_DIM + MLA_V_DIM)), MLA_KV_RANK),
        "na_rpb": 0.1 * jax.random.normal(ks[8], (L, NA_HEADS, 2 * NA_ROWS_MAX - 1, 2 * NA_COLS - 1), jnp.float32),
        "na_out_norm": gain(ks[9], (L, NA_WIDTH)),
        "mla_out_norm": gain(ks[10], (L, MLA_WIDTH)),
        "w_out": dense(ks[11], (L, D_MIX, D_MODEL), D_MIX),
        "ln_mem": gain(ks[12], (L, D_MODEL)),
        "mem_norm": gain(ks[13], (L, D_MODEL)),
        "w_xq": dense(ks[14], (L, D_MODEL, D_MODEL), D_MODEL),
        "w_xk": dense(ks[15], (L, D_MODEL, D_MODEL), D_MODEL),
        "w_xv": dense(ks[16], (L, D_MODEL, D_MODEL), D_MODEL),
        "w_xo": dense(ks[17], (L, D_MODEL, D_MODEL), D_MODEL),
        "ln_ffn": gain(ks[18], (L, D_MODEL)),
        "w_gate": dense(ks[19], (L, D_MODEL, D_FF), D_MODEL),
        "w_up": dense(ks[20], (L, D_MODEL, D_FF), D_MODEL),
        "w_down": dense(ks[21], (L, D_FF, D_MODEL), D_FF),
        "final_norm": gain(ks[22], (D_MODEL,)),
    }


def reference(x, mem, ln_mix, w_in, q_norm, kv_norm, w_uq, w_ukv, na_rpb, na_out_norm,
              mla_out_norm, w_out, ln_mem, mem_norm, w_xq, w_xk, w_xv, w_xo, ln_ffn,
              w_gate, w_up, w_down, final_norm):
    B, S, _ = x.shape
    cos, sin = rope_tables(S)
    o1 = NA_WIDTH
    o2 = 2 * NA_WIDTH
    o3 = 3 * NA_WIDTH
    o4 = o3 + MLA_Q_RANK
    o5 = o4 + MLA_KV_RANK
    for l in range(DEPTH):
        h = rmsnorm(x, ln_mix[l])
        proj = h @ w_in[l]
        q_na = proj[..., :o1].reshape(B, S, NA_HEADS, NA_HEAD_DIM)
        k_na = proj[..., o1:o2].reshape(B, S, NA_HEADS, NA_HEAD_DIM)
        v_na = proj[..., o2:o3].reshape(B, S, NA_HEADS, NA_HEAD_DIM)
        c_q = proj[..., o3:o4]
        c_kv = proj[..., o4:o5]
        k_rope = proj[..., o5:]
        y_na = neighbourhood_attention(q_na, k_na, v_na, na_rpb[l])
        y_mla = latent_attention(c_q, c_kv, k_rope, q_norm[l], kv_norm[l], w_uq[l], w_ukv[l], cos, sin)
        y = jnp.concatenate([rmsnorm(y_na, na_out_norm[l]), rmsnorm(y_mla, mla_out_norm[l])], axis=-1)
        x = x + y @ w_out[l]
        h = rmsnorm(x, ln_mem[l])
        x = x + memory_attention(h, rmsnorm(mem, mem_norm[l]), w_xq[l], w_xk[l], w_xv[l], w_xo[l])
        h = rmsnorm(x, ln_ffn[l])
        x = x + swiglu(h, w_gate[l], w_up[l], w_down[l])
    return rmsnorm(x, final_norm)
```

```python
import functools

import jax
import jax.numpy as jnp
import numpy as np
from jax import lax
from jax.experimental import pallas as pl
from jax.experimental.pallas import tpu as pltpu

F32 = jnp.float32
BF16 = jnp.bfloat16

EPS = 1e-6
GRID_W = 64
NA_HEADS = 8
NA_HEAD_DIM = 128
NA_WIDTH = NA_HEADS * NA_HEAD_DIM
NA_ROWS = 8
NA_COLS = 16
MLA_HEADS = 8
MLA_Q_RANK = 512
MLA_KV_RANK = 256
MLA_NOPE = 128
MLA_ROPE = 64
MLA_V = 128
MLA_QK = MLA_NOPE + MLA_ROPE
MLA_QK_PAD = 256
MLA_WIDTH = MLA_HEADS * MLA_V
ROPE_THETA = 10000.0
X_HEADS = 4
LANES = 128
VMEM_LIMIT = 56 * 1024 * 1024

NA_BLOCK_ROWS = 8
NA_WIN_ROWS = 16
NEG_INF = -1e30


def _cparams(sem):
    return pltpu.CompilerParams(dimension_semantics=sem, vmem_limit_bytes=VMEM_LIMIT)


def _nt_dot(a, b):
    return lax.dot_general(a, b, (((1,), (1,)), ((), ())), preferred_element_type=F32)


def _rms(x, g):
    return x * lax.rsqrt(jnp.mean(x * x, axis=-1, keepdims=True) + EPS) * g


def _nmm_kernel(*refs, widths, normed, has_res):
    n = len(widths)
    x_refs = refs[:n]
    g_refs = refs[n:n + sum(normed)]
    rest = refs[n + sum(normed):]
    w_ref = rest[0]
    r_ref = rest[1] if has_res else None
    o_ref = rest[1 + has_res]
    xn_ref = rest[2 + has_res]

    @pl.when(pl.program_id(1) == 0)
    def _():
        off = 0
        gi = 0
        for p in range(n):
            x = x_refs[p][...]
            if normed[p]:
                x = _rms(x.astype(F32), g_refs[gi][...])
                gi += 1
            xn_ref[:, off:off + widths[p]] = x.astype(BF16)
            off += widths[p]

    acc = jnp.dot(xn_ref[...], w_ref[...], preferred_element_type=F32)
    if has_res:
        acc = acc + r_ref[...]
    o_ref[...] = acc.astype(o_ref.dtype)


def _nmm(parts, gains, w, res=None, out_dtype=F32, tm=1024, tn=512, split=1):
    m = parts[0].shape[0]
    widths = tuple(p.shape[1] for p in parts)
    k, n = w.shape
    assert sum(widths) == k
    tm = min(tm, m)
    tn = min(tn, n // split)
    assert m % tm == 0 and n % (split * tn) == 0
    per = n // split // tn
    if split > 1:
        out_spec = pl.BlockSpec((None, tm, tn), lambda i, j: (j // per, i, j % per))
        out_shape = jax.ShapeDtypeStruct((split, m, n // split), out_dtype)
    else:
        out_spec = pl.BlockSpec((tm, tn), lambda i, j: (i, j))
        out_shape = jax.ShapeDtypeStruct((m, n), out_dtype)
    normed = tuple(g is not None for g in gains)
    in_specs = [pl.BlockSpec((tm, wd), lambda i, j: (i, 0)) for wd in widths]
    args = list(parts)
    for g, wd in zip(gains, widths):
        if g is not None:
            in_specs.append(pl.BlockSpec((1, wd), lambda i, j: (0, 0)))
            args.append(g.reshape(1, wd).astype(F32))
    in_specs.append(pl.BlockSpec((k, tn), lambda i, j: (0, j)))
    args.append(w)
    if res is not None:
        in_specs.append(pl.BlockSpec((tm, tn), lambda i, j: (i, j)))
        args.append(res)
    return pl.pallas_call(
        functools.partial(_nmm_kernel, widths=widths, normed=normed, has_res=res is not None),
        grid=(m // tm, n // tn),
        in_specs=in_specs,
        out_specs=out_spec,
        out_shape=out_shape,
        scratch_shapes=[pltpu.VMEM((tm, k), BF16)],
        compiler_params=_cparams(("parallel", "arbitrary")),
        name="nmm",
    )(*args)


def _na_kernel(q_ref, k_ref, v_ref, b_ref, o_ref, *, rows, scale):
    blk = pl.program_id(0)
    win0 = jnp.clip(NA_BLOCK_ROWS * blk - NA_ROWS // 2, 0, rows - NA_WIN_ROWS)

    def row_body(i, carry):
        r = blk * NA_BLOCK_ROWS + i
        rs = jnp.clip(r - NA_ROWS // 2, 0, rows - NA_ROWS)
        koff = pl.multiple_of((rs - win0) * GRID_W, GRID_W)
        dr0 = rs - r + (NA_ROWS - 1)
        qoff = pl.multiple_of(i * GRID_W, GRID_W)
        for h in range(NA_HEADS):
            hs = slice(h * NA_HEAD_DIM, (h + 1) * NA_HEAD_DIM)
            q = q_ref[pl.ds(qoff, GRID_W), hs]
            k = k_ref[0, pl.ds(koff, NA_ROWS * GRID_W), hs]
            v = v_ref[0, pl.ds(koff, NA_ROWS * GRID_W), hs]
            s = _nt_dot(q, k) * scale + b_ref[h, dr0]
            m = jnp.max(s, axis=-1, keepdims=True)
            p = jnp.exp(s - m)
            l = jnp.sum(p, axis=-1, keepdims=True)
            o = jnp.dot(p.astype(BF16), v, preferred_element_type=F32)
            o_ref[pl.ds(qoff, GRID_W), hs] = o / l
        return carry

    lax.fori_loop(0, NA_BLOCK_ROWS, row_body, 0)


def _na_bias_table(rpb):
    qc = np.arange(GRID_W)
    col_start = np.clip(qc - NA_COLS // 2, 0, GRID_W - NA_COLS)
    kc = np.arange(GRID_W)
    inside = (kc[None, :] >= col_start[:, None]) & (kc[None, :] < col_start[:, None] + NA_COLS)
    dc = np.clip(kc[None, :] - qc[:, None] + (NA_COLS - 1), 0, 2 * NA_COLS - 2)
    full = jnp.where(inside[None, None], rpb[:, :, dc], NEG_INF)
    dr = np.arange(NA_ROWS)[:, None] + np.arange(NA_ROWS)[None, :]
    t = full[:, dr]
    t = jnp.transpose(t, (0, 1, 3, 2, 4))
    return t.reshape(NA_HEADS, NA_ROWS, GRID_W, NA_ROWS * GRID_W).astype(F32)


def _na_attention(qkv, bias):
    s = qkv.shape[1]
    rows = s // GRID_W
    assert rows % NA_BLOCK_ROWS == 0 and rows >= NA_WIN_ROWS
    tq = NA_BLOCK_ROWS * GRID_W
    win = NA_WIN_ROWS * GRID_W

    def kv_map(which):
        def index(b):
            r0 = jnp.clip(NA_BLOCK_ROWS * b - NA_ROWS // 2, 0, rows - NA_WIN_ROWS)
            return (which, r0 * GRID_W, 0)
        return index

    kv_block = (pl.Element(1), pl.Element(win), pl.Element(NA_WIDTH))
    return pl.pallas_call(
        functools.partial(_na_kernel, rows=rows, scale=NA_HEAD_DIM ** -0.5),
        grid=(rows // NA_BLOCK_ROWS,),
        in_specs=[
            pl.BlockSpec((None, tq, NA_WIDTH), lambda b: (0, b, 0)),
            pl.BlockSpec(kv_block, kv_map(1)),
            pl.BlockSpec(kv_block, kv_map(2)),
            pl.BlockSpec(bias.shape, lambda b: (0, 0, 0, 0)),
        ],
        out_specs=pl.BlockSpec((tq, NA_WIDTH), lambda b: (b, 0)),
        out_shape=jax.ShapeDtypeStruct((s, NA_WIDTH), F32),
        compiler_params=_cparams(("parallel",)),
        name="na_attention",
    )(qkv, qkv, qkv, bias)


def _latup_kernel(c_ref, qg_ref, kvg_ref, wq_ref, wqs_ref, wk_ref, wv_ref, tab_ref,
                  q_ref, k_ref, v_ref, *, scale):
    c = c_ref[...]
    qn = _rms(c[:, :MLA_Q_RANK], qg_ref[...]).astype(BF16)
    kvn = _rms(c[:, MLA_Q_RANK:MLA_Q_RANK + MLA_KV_RANK], kvg_ref[...]).astype(BF16)
    a = jnp.dot(qn, wq_ref[...], preferred_element_type=F32)
    b = jnp.dot(qn, wqs_ref[...], preferred_element_type=F32)
    kn = jnp.dot(kvn, wk_ref[...], preferred_element_type=F32)
    v_ref[...] = jnp.dot(kvn, wv_ref[...], preferred_element_type=F32).astype(BF16)
    tab = tab_ref[...]
    cq, sq = tab[:, :LANES], tab[:, LANES:2 * LANES]
    ck, sk = tab[:, 2 * LANES:3 * LANES], tab[:, 3 * LANES:]
    r0 = MLA_Q_RANK + MLA_KV_RANK
    kr = (c[:, r0:r0 + LANES] * ck + c[:, r0 + LANES:r0 + 2 * LANES] * sk).astype(BF16)
    for h in range(MLA_HEADS):
        o = h * MLA_QK_PAD
        q_ref[:, o:o + LANES] = (a[:, o:o + LANES] * scale).astype(BF16)
        q_ref[:, o + LANES:o + 2 * LANES] = (
            a[:, o + LANES:o + 2 * LANES] * cq + b[:, h * LANES:(h + 1) * LANES] * sq).astype(BF16)
        k_ref[:, o:o + LANES] = kn[:, h * LANES:(h + 1) * LANES].astype(BF16)
        k_ref[:, o + LANES:o + 2 * LANES] = kr


def _latent_up(c, q_gain, kv_gain, wq, wqs, wk, wv, tab, tm=512):
    s = c.shape[0]
    tm = min(tm, s)
    const = lambda i: (0, 0)
    row = lambda i: (i, 0)
    return pl.pallas_call(
        functools.partial(_latup_kernel, scale=MLA_QK ** -0.5),
        grid=(s // tm,),
        in_specs=[
            pl.BlockSpec((tm, c.shape[1]), row),
            pl.BlockSpec((1, MLA_Q_RANK), const),
            pl.BlockSpec((1, MLA_KV_RANK), const),
            pl.BlockSpec(wq.shape, const),
            pl.BlockSpec(wqs.shape, const),
            pl.BlockSpec(wk.shape, const),
            pl.BlockSpec(wv.shape, const),
            pl.BlockSpec((tm, 4 * LANES), row),
        ],
        out_specs=[
            pl.BlockSpec((tm, MLA_HEADS * MLA_QK_PAD), row),
            pl.BlockSpec((tm, MLA_HEADS * MLA_QK_PAD), row),
            pl.BlockSpec((tm, MLA_WIDTH), row),
        ],
        out_shape=[
            jax.ShapeDtypeStruct((s, MLA_HEADS * MLA_QK_PAD), BF16),
            jax.ShapeDtypeStruct((s, MLA_HEADS * MLA_QK_PAD), BF16),
            jax.ShapeDtypeStruct((s, MLA_WIDTH), BF16),
        ],
        compiler_params=_cparams(("parallel",)),
        name="latent_up",
    )(c, q_gain.reshape(1, -1), kv_gain.reshape(1, -1), wq, wqs, wk, wv, tab)


def _rope_tables(s, scale):
    half = MLA_ROPE // 2
    inv = 1.0 / (ROPE_THETA ** (jnp.arange(0, MLA_ROPE, 2, dtype=F32) / MLA_ROPE))
    ang = jnp.arange(s, dtype=F32)[:, None] * inv[None, :]
    cos, sin = jnp.cos(ang), jnp.sin(ang)
    z = jnp.zeros((s, LANES - 2 * half), F32)
    ct = jnp.concatenate([cos, cos, z], axis=1)
    st = jnp.concatenate([-sin, sin, z], axis=1)
    return jnp.concatenate([ct * scale, st * scale, ct, st], axis=1)


def _mla_kernel(q_ref, k_ref, v_ref, o_ref, m_ref, l_ref, acc_ref, *, tk):
    nk = k_ref.shape[0] // tk
    m_ref[...] = jnp.full(m_ref.shape, -jnp.inf, F32)
    l_ref[...] = jnp.zeros(l_ref.shape, F32)
    acc_ref[...] = jnp.zeros(acc_ref.shape, F32)
    q = q_ref[...]

    def body(j, carry):
        off = pl.multiple_of(j * tk, tk)
        s = _nt_dot(q, k_ref[pl.ds(off, tk), :])
        m_prev = m_ref[...]
        m_new = jnp.maximum(m_prev, jnp.max(s, axis=-1, keepdims=True))
        alpha = jnp.exp(m_prev - m_new)
        p = jnp.exp(s - m_new)
        l_ref[...] = alpha * l_ref[...] + jnp.sum(p, axis=-1, keepdims=True)
        acc_ref[...] = alpha * acc_ref[...] + jnp.dot(
            p.astype(BF16), v_ref[pl.ds(off, tk), :], preferred_element_type=F32)
        m_ref[...] = m_new
        return carry

    lax.fori_loop(0, nk, body, 0)
    o_ref[...] = acc_ref[...] / l_ref[...]


def _mla_attention(q, k, v, tq=512, tk=512):
    s = q.shape[0]
    tq = min(tq, s)
    tk = min(tk, s)
    return pl.pallas_call(
        functools.partial(_mla_kernel, tk=tk),
        grid=(MLA_HEADS, s // tq),
        in_specs=[
            pl.BlockSpec((tq, MLA_QK_PAD), lambda h, i: (i, h)),
            pl.BlockSpec((s, MLA_QK_PAD), lambda h, i: (0, h)),
            pl.BlockSpec((s, MLA_V), lambda h, i: (0, h)),
        ],
        out_specs=pl.BlockSpec((tq, MLA_V), lambda h, i: (i, h)),
        out_shape=jax.ShapeDtypeStruct((s, MLA_WIDTH), F32),
        scratch_shapes=[
            pltpu.VMEM((tq, 1), F32),
            pltpu.VMEM((tq, 1), F32),
            pltpu.VMEM((tq, MLA_V), F32),
        ],
        compiler_params=_cparams(("parallel", "parallel")),
        name="mla_attention",
    )(q, k, v)


def _xattn_kernel(q_ref, kv_ref, o_ref, *, d_model, scale):
    dh = d_model // X_HEADS
    for h in range(X_HEADS):
        q = q_ref[:, h * dh:(h + 1) * dh]
        k = kv_ref[:, h * dh:(h + 1) * dh]
        v = kv_ref[:, d_model + h * dh:d_model + (h + 1) * dh]
        s = _nt_dot(q, k) * scale
        m = jnp.max(s, axis=-1, keepdims=True)
        p = jnp.exp(s - m)
        l = jnp.sum(p, axis=-1, keepdims=True)
        o = jnp.dot(p.astype(BF16), v, preferred_element_type=F32) / l
        o_ref[:, h * dh:(h + 1) * dh] = o.astype(o_ref.dtype)


def _xattn(q, kv, tq=512):
    s, d_model = q.shape
    tq = min(tq, s)
    return pl.pallas_call(
        functools.partial(_xattn_kernel, d_model=d_model, scale=(d_model // X_HEADS) ** -0.5),
        grid=(s // tq,),
        in_specs=[
            pl.BlockSpec((tq, d_model), lambda i: (i, 0)),
            pl.BlockSpec(kv.shape, lambda i: (0, 0)),
        ],
        out_specs=pl.BlockSpec((tq, d_model), lambda i: (i, 0)),
        out_shape=jax.ShapeDtypeStruct((s, d_model), BF16),
        compiler_params=_cparams(("parallel",)),
        name="mem_attention",
    )(q, kv)


def _ffn_kernel(x_ref, g_ref, wg_ref, wu_ref, wd_ref, o_ref, xn_ref):
    f = pl.program_id(1)

    @pl.when(f == 0)
    def _():
        x = x_ref[...]
        xn_ref[...] = _rms(x, g_ref[...]).astype(BF16)
        o_ref[...] = x

    xn = xn_ref[...]
    gate = jnp.dot(xn, wg_ref[...], preferred_element_type=F32)
    up = jnp.dot(xn, wu_ref[...], preferred_element_type=F32)
    act = (gate * jax.nn.sigmoid(gate) * up).astype(BF16)
    o_ref[...] += jnp.dot(act, wd_ref[...], preferred_element_type=F32)


def _ffn(x, gain, wg, wu, wd, tm=1024, tf=512):
    s, d = x.shape
    f = wg.shape[1]
    tm = min(tm, s)
    assert s % tm == 0 and f % tf == 0
    return pl.pallas_call(
        _ffn_kernel,
        grid=(s // tm, f // tf),
        in_specs=[
            pl.BlockSpec((tm, d), lambda i, j: (i, 0), pipeline_mode=pl.Buffered(1)),
            pl.BlockSpec((1, d), lambda i, j: (0, 0)),
            pl.BlockSpec((d, tf), lambda i, j: (0, j)),
            pl.BlockSpec((d, tf), lambda i, j: (0, j)),
            pl.BlockSpec((tf, d), lambda i, j: (j, 0)),
        ],
        out_specs=pl.BlockSpec((tm, d), lambda i, j: (i, 0)),
        out_shape=jax.ShapeDtypeStruct((s, d), F32),
        scratch_shapes=[pltpu.VMEM((tm, d), BF16)],
        compiler_params=_cparams(("parallel", "arbitrary")),
        name="ffn",
    )(x, gain.reshape(1, d), wg, wu, wd)


def _norm_kernel(x_ref, g_ref, o_ref):
    o_ref[...] = _rms(x_ref[...], g_ref[...])


def _final_norm(x, gain, tm=512):
    s, d = x.shape
    tm = min(tm, s)
    return pl.pallas_call(
        _norm_kernel,
        grid=(s // tm,),
        in_specs=[pl.BlockSpec((tm, d), lambda i: (i, 0)), pl.BlockSpec((1, d), lambda i: (0, 0))],
        out_specs=pl.BlockSpec((tm, d), lambda i: (i, 0)),
        out_shape=jax.ShapeDtypeStruct((s, d), F32),
        compiler_params=_cparams(("parallel",)),
        name="final_norm",
    )(x, gain.reshape(1, d))


def _split_w_in(w_in):
    d = w_in.shape[0]
    o3 = 3 * NA_WIDTH
    o5 = o3 + MLA_Q_RANK + MLA_KV_RANK
    half = MLA_ROPE // 2
    z = jnp.zeros((d, LANES - MLA_ROPE), w_in.dtype)
    w_lat = jnp.concatenate(
        [w_in[:, o3:], z, w_in[:, o5 + half:], w_in[:, o5:o5 + half], z], axis=1)
    return w_in[:, :o3].astype(BF16), w_lat.astype(BF16)


def _split_w_uq(w_uq):
    r = w_uq.shape[0]
    half = MLA_ROPE // 2
    w = w_uq.reshape(r, MLA_HEADS, MLA_QK)
    wp = jnp.pad(w, ((0, 0), (0, 0), (0, MLA_QK_PAD - MLA_QK)))
    ws = jnp.concatenate(
        [w[:, :, MLA_NOPE + half:], w[:, :, MLA_NOPE:MLA_NOPE + half],
         jnp.zeros((r, MLA_HEADS, LANES - MLA_ROPE), w.dtype)], axis=2)
    return (wp.reshape(r, MLA_HEADS * MLA_QK_PAD).astype(BF16),
            ws.reshape(r, MLA_HEADS * LANES).astype(BF16))


def _split_w_ukv(w_ukv):
    r = w_ukv.shape[0]
    w = w_ukv.reshape(r, MLA_HEADS, MLA_NOPE + MLA_V)
    return (w[:, :, :MLA_NOPE].reshape(r, MLA_HEADS * MLA_NOPE).astype(BF16),
            w[:, :, MLA_NOPE:].reshape(r, MLA_HEADS * MLA_V).astype(BF16))


def kernel(x, mem, ln_mix, w_in, q_norm, kv_norm, w_uq, w_ukv, na_rpb, na_out_norm, mla_out_norm,
           w_out, ln_mem, mem_norm, w_xq, w_xk, w_xv, w_xo, ln_ffn, w_gate, w_up, w_down,
           final_norm):
    b, s, d = x.shape
    assert b == 1
    depth = w_in.shape[0]
    x = x.reshape(s, d)
    mem2 = mem.reshape(mem.shape[1], d)
    tab = _rope_tables(s, MLA_QK ** -0.5)
    for l in range(depth):
        w_qkv, w_lat = _split_w_in(w_in[l])
        wq, wqs = _split_w_uq(w_uq[l])
        wk, wv = _split_w_ukv(w_ukv[l])
        qkv = _nmm([x], [ln_mix[l]], w_qkv, out_dtype=BF16, split=3)
        c = _nmm([x], [ln_mix[l]], w_lat, out_dtype=F32)
        y_na = _na_attention(qkv, _na_bias_table(na_rpb[l]))
        ql, kl, vl = _latent_up(c, q_norm[l], kv_norm[l], wq, wqs, wk, wv, tab)
        y_mla = _mla_attention(ql, kl, vl)
        x = _nmm([y_na, y_mla], [na_out_norm[l], mla_out_norm[l]], w_out[l].astype(BF16), res=x)
        qx = _nmm([x], [ln_mem[l]], w_xq[l].astype(BF16), out_dtype=BF16)
        w_kv = jnp.concatenate([w_xk[l], w_xv[l]], axis=1).astype(BF16)
        kvx = _nmm([mem2], [mem_norm[l]], w_kv, out_dtype=BF16)
        ox = _xattn(qx, kvx)
        x = _nmm([ox], [None], w_xo[l].astype(BF16), res=x)
        x = _ffn(x, ln_ffn[l], w_gate[l].astype(BF16), w_up[l].astype(BF16),
                 w_down[l].astype(BF16))
    return _final_norm(x, final_norm).reshape(b, s, d)
```

```python
import functools

import jax
import jax.numpy as jnp
import numpy as np
from jax import lax
from jax.experimental import pallas as pl
from jax.experimental.pallas import tpu as pltpu

F32 = jnp.float32
BF16 = jnp.bfloat16

EPS = 1e-6
GRID_W = 64
NA_HEADS = 8
NA_HEAD_DIM = 128
NA_WIDTH = NA_HEADS * NA_HEAD_DIM
NA_ROWS = 8
NA_COLS = 16
MLA_HEADS = 8
MLA_Q_RANK = 512
MLA_KV_RANK = 256
MLA_NOPE = 128
MLA_ROPE = 64
MLA_V = 128
MLA_QK = MLA_NOPE + MLA_ROPE
MLA_QK_PAD = 256
MLA_WIDTH = MLA_HEADS * MLA_V
MLA_SCORE_SCALE = MLA_QK ** -0.5 * float(np.log2(np.e))
MLA_KV_CHUNK = 1024
ROPE_THETA = 10000.0
X_HEADS = 4
LANES = 128
VMEM_LIMIT = 56 * 1024 * 1024

NA_BLOCK_ROWS = 8
NA_WIN_ROWS = 16
NEG_INF = -1e30


def _cparams(sem):
    return pltpu.CompilerParams(dimension_semantics=sem, vmem_limit_bytes=VMEM_LIMIT)


def _nt_dot(a, b):
    return lax.dot_general(a, b, (((1,), (1,)), ((), ())), preferred_element_type=F32)


def _rms(x, g):
    return x * lax.rsqrt(jnp.mean(x * x, axis=-1, keepdims=True) + EPS) * g


def _nmm_kernel(*refs, widths, normed, splits, has_res, sub):
    n = len(widths)
    nw = len(splits)
    x_refs = refs[:n]
    g_refs = refs[n:n + sum(normed)]
    rest = refs[n + sum(normed):]
    w_refs = rest[:nw]
    r_ref = rest[nw] if has_res else None
    o_refs = rest[nw + has_res:]
    tm = x_refs[0].shape[0]
    for r0 in range(0, tm, sub):
        rows = slice(r0, r0 + sub)
        xs = []
        gi = 0
        for p in range(n):
            x = x_refs[p][rows, :]
            if normed[p]:
                x = _rms(x.astype(F32), g_refs[gi][...])
                gi += 1
            xs.append(x.astype(BF16))
        for w_ref, o_ref, split in zip(w_refs, o_refs, splits):
            off = 0
            acc = None
            for p in range(n):
                d = jnp.dot(xs[p], w_ref[off:off + widths[p], :], preferred_element_type=F32)
                acc = d if acc is None else acc + d
                off += widths[p]
            if has_res:
                acc = acc + r_ref[rows, :]
            if split == 1:
                o_ref[rows, :] = acc.astype(o_ref.dtype)
            else:
                wd = acc.shape[1] // split
                for c in range(split):
                    o_ref[c, rows, :] = acc[:, c * wd:(c + 1) * wd].astype(o_ref.dtype)


def _nmm(parts, gains, ws, res=None, out_dtypes=(F32,), splits=(1,), tm=512, sub=256):
    m = parts[0].shape[0]
    widths = tuple(p.shape[1] for p in parts)
    k = sum(widths)
    assert all(w.shape[0] == k for w in ws) and (res is None or len(ws) == 1)
    tm = min(tm, m)
    sub = min(sub, tm)
    assert m % tm == 0 and tm % sub == 0
    normed = tuple(g is not None for g in gains)
    row = lambda i: (i, 0)
    const = lambda i: (0, 0)
    in_specs = [pl.BlockSpec((tm, wd), row) for wd in widths]
    args = list(parts)
    for g, wd in zip(gains, widths):
        if g is not None:
            in_specs.append(pl.BlockSpec((1, wd), const))
            args.append(g.reshape(1, wd).astype(F32))
    for w in ws:
        in_specs.append(pl.BlockSpec(w.shape, const, pipeline_mode=pl.Buffered(1)))
        args.append(w)
    if res is not None:
        in_specs.append(pl.BlockSpec((tm, ws[0].shape[1]), row))
        args.append(res)
    out_specs, out_shapes = [], []
    for w, dt, split in zip(ws, out_dtypes, splits):
        n = w.shape[1]
        if split > 1:
            out_specs.append(pl.BlockSpec((split, tm, n // split), lambda i: (0, i, 0)))
            out_shapes.append(jax.ShapeDtypeStruct((split, m, n // split), dt))
        else:
            out_specs.append(pl.BlockSpec((tm, n), row))
            out_shapes.append(jax.ShapeDtypeStruct((m, n), dt))
    outs = pl.pallas_call(
        functools.partial(_nmm_kernel, widths=widths, normed=normed, splits=tuple(splits),
                          has_res=res is not None, sub=sub),
        grid=(m // tm,),
        in_specs=in_specs,
        out_specs=out_specs,
        out_shape=out_shapes,
        compiler_params=_cparams(("parallel",)),
        name="nmm",
    )(*args)
    return outs if len(outs) > 1 else outs[0]


def _na_kernel(q_ref, k_ref, v_ref, b_ref, o_ref, *, rows, scale):
    blk = pl.program_id(0)
    win0 = jnp.clip(NA_BLOCK_ROWS * blk - NA_ROWS // 2, 0, rows - NA_WIN_ROWS)

    def row_body(i, carry):
        r = blk * NA_BLOCK_ROWS + i
        rs = jnp.clip(r - NA_ROWS // 2, 0, rows - NA_ROWS)
        koff = pl.multiple_of((rs - win0) * GRID_W, GRID_W)
        dr0 = rs - r + (NA_ROWS - 1)
        qoff = pl.multiple_of(i * GRID_W, GRID_W)
        for h in range(NA_HEADS):
            hs = slice(h * NA_HEAD_DIM, (h + 1) * NA_HEAD_DIM)
            q = q_ref[pl.ds(qoff, GRID_W), hs]
            k = k_ref[0, pl.ds(koff, NA_ROWS * GRID_W), hs]
            v = v_ref[0, pl.ds(koff, NA_ROWS * GRID_W), hs]
            s = _nt_dot(q, k) * scale + b_ref[h, dr0]
            m = jnp.max(s, axis=-1, keepdims=True)
            p = jnp.exp(s - m)
            l = jnp.sum(p, axis=-1, keepdims=True)
            o = jnp.dot(p.astype(BF16), v, preferred_element_type=F32)
            o_ref[pl.ds(qoff, GRID_W), hs] = o / l
        return carry

    lax.fori_loop(0, NA_BLOCK_ROWS, row_body, 0)


def _na_bias_table(rpb):
    qc = np.arange(GRID_W)
    col_start = np.clip(qc - NA_COLS // 2, 0, GRID_W - NA_COLS)
    kc = np.arange(GRID_W)
    inside = (kc[None, :] >= col_start[:, None]) & (kc[None, :] < col_start[:, None] + NA_COLS)
    dc = np.clip(kc[None, :] - qc[:, None] + (NA_COLS - 1), 0, 2 * NA_COLS - 2)
    full = jnp.where(inside[None, None], rpb[:, :, dc], NEG_INF)
    dr = np.arange(NA_ROWS)[:, None] + np.arange(NA_ROWS)[None, :]
    t = full[:, dr]
    t = jnp.transpose(t, (0, 1, 3, 2, 4))
    return t.reshape(NA_HEADS, NA_ROWS, GRID_W, NA_ROWS * GRID_W).astype(F32)


def _na_attention(qkv, bias):
    s = qkv.shape[1]
    rows = s // GRID_W
    assert rows % NA_BLOCK_ROWS == 0 and rows >= NA_WIN_ROWS
    tq = NA_BLOCK_ROWS * GRID_W
    win = NA_WIN_ROWS * GRID_W

    def kv_map(which):
        def index(b):
            r0 = jnp.clip(NA_BLOCK_ROWS * b - NA_ROWS // 2, 0, rows - NA_WIN_ROWS)
            return (which, r0 * GRID_W, 0)
        return index

    kv_block = (pl.Element(1), pl.Element(win), pl.Element(NA_WIDTH))
    return pl.pallas_call(
        functools.partial(_na_kernel, rows=rows, scale=NA_HEAD_DIM ** -0.5),
        grid=(rows // NA_BLOCK_ROWS,),
        in_specs=[
            pl.BlockSpec((None, tq, NA_WIDTH), lambda b: (0, b, 0)),
            pl.BlockSpec(kv_block, kv_map(1)),
            pl.BlockSpec(kv_block, kv_map(2)),
            pl.BlockSpec(bias.shape, lambda b: (0, 0, 0, 0)),
        ],
        out_specs=pl.BlockSpec((tq, NA_WIDTH), lambda b: (b, 0)),
        out_shape=jax.ShapeDtypeStruct((s, NA_WIDTH), F32),
        compiler_params=_cparams(("parallel",)),
        name="na_attention",
    )(qkv, qkv, qkv, bias)


def _latup_kernel(c_ref, qg_ref, kvg_ref, wq_ref, wqs_ref, wk_ref, wv_ref, tab_ref,
                  q_ref, k_ref, v_ref, *, scale):
    c = c_ref[...]
    qn = _rms(c[:, :MLA_Q_RANK], qg_ref[...]).astype(BF16)
    kvn = _rms(c[:, MLA_Q_RANK:MLA_Q_RANK + MLA_KV_RANK], kvg_ref[...]).astype(BF16)
    a = jnp.dot(qn, wq_ref[...], preferred_element_type=F32)
    b = jnp.dot(qn, wqs_ref[...], preferred_element_type=F32)
    kn = jnp.dot(kvn, wk_ref[...], preferred_element_type=F32)
    v_ref[...] = _nt_dot(wv_ref[...], kvn).astype(BF16)
    tab = tab_ref[...]
    cq, sq = tab[:, :LANES], tab[:, LANES:2 * LANES]
    ck, sk = tab[:, 2 * LANES:3 * LANES], tab[:, 3 * LANES:]
    r0 = MLA_Q_RANK + MLA_KV_RANK
    kr = (c[:, r0:r0 + LANES] * ck + c[:, r0 + LANES:r0 + 2 * LANES] * sk).astype(BF16)
    for h in range(MLA_HEADS):
        o = h * MLA_QK_PAD
        q_ref[:, o:o + LANES] = (a[:, o:o + LANES] * scale).astype(BF16)
        q_ref[:, o + LANES:o + 2 * LANES] = (
            a[:, o + LANES:o + 2 * LANES] * cq + b[:, h * LANES:(h + 1) * LANES] * sq).astype(BF16)
        k_ref[:, o:o + LANES] = kn[:, h * LANES:(h + 1) * LANES].astype(BF16)
        k_ref[:, o + LANES:o + 2 * LANES] = kr


def _latent_up(c, q_gain, kv_gain, wq, wqs, wk, wvt, tab, tm):
    s = c.shape[0]
    assert s % tm == 0
    const = lambda i: (0, 0)
    row = lambda i: (i, 0)
    return pl.pallas_call(
        functools.partial(_latup_kernel, scale=MLA_SCORE_SCALE),
        grid=(s // tm,),
        in_specs=[
            pl.BlockSpec((tm, c.shape[1]), row),
            pl.BlockSpec((1, MLA_Q_RANK), const),
            pl.BlockSpec((1, MLA_KV_RANK), const),
            pl.BlockSpec(wq.shape, const),
            pl.BlockSpec(wqs.shape, const),
            pl.BlockSpec(wk.shape, const),
            pl.BlockSpec(wvt.shape, const),
            pl.BlockSpec((tm, 4 * LANES), row),
        ],
        out_specs=[
            pl.BlockSpec((tm, MLA_HEADS * MLA_QK_PAD), row),
            pl.BlockSpec((tm, MLA_HEADS * MLA_QK_PAD), row),
            pl.BlockSpec((None, MLA_WIDTH, tm), lambda i: (i, 0, 0)),
        ],
        out_shape=[
            jax.ShapeDtypeStruct((s, MLA_HEADS * MLA_QK_PAD), BF16),
            jax.ShapeDtypeStruct((s, MLA_HEADS * MLA_QK_PAD), BF16),
            jax.ShapeDtypeStruct((s // tm, MLA_WIDTH, tm), BF16),
        ],
        compiler_params=_cparams(("parallel",)),
        name="latent_up",
    )(c, q_gain.reshape(1, -1), kv_gain.reshape(1, -1), wq, wqs, wk, wvt, tab)


def _rope_tables(s, scale):
    half = MLA_ROPE // 2
    inv = 1.0 / (ROPE_THETA ** (jnp.arange(0, MLA_ROPE, 2, dtype=F32) / MLA_ROPE))
    ang = jnp.arange(s, dtype=F32)[:, None] * inv[None, :]
    cos, sin = jnp.cos(ang), jnp.sin(ang)
    z = jnp.zeros((s, LANES - 2 * half), F32)
    ct = jnp.concatenate([cos, cos, z], axis=1)
    st = jnp.concatenate([-sin, sin, z], axis=1)
    return jnp.concatenate([ct * scale, st * scale, ct, st], axis=1)


def _mla_kernel(q_ref, k_ref, vt_ref, o_ref, acc_ref, *, unroll):
    nk, _, tk = vt_ref.shape
    tq = q_ref.shape[0]
    q = q_ref[...]
    acc_ref[...] = jnp.zeros(acc_ref.shape, F32)

    def body(j, carry):
        m_prev, l_prev = carry
        off = pl.multiple_of(j * tk, tk)
        s = _nt_dot(k_ref[pl.ds(off, tk), :], q)
        m_new = jnp.maximum(m_prev, jnp.max(s, axis=0, keepdims=True))
        alpha = jnp.exp2(m_prev - m_new)
        p = jnp.exp2(s - m_new)
        l_new = alpha * l_prev + jnp.sum(p, axis=0, keepdims=True)
        acc_ref[...] = alpha * acc_ref[...] + jnp.dot(
            vt_ref[j], p.astype(BF16), preferred_element_type=F32)
        return m_new, l_new

    init = (jnp.full((1, tq), -jnp.inf, F32), jnp.zeros((1, tq), F32))
    _, l = lax.fori_loop(0, nk, body, init, unroll=unroll)
    o_ref[...] = (acc_ref[...] / l).T


def _mla_attention(q, k, vt, tq=1024, unroll=2):
    s = q.shape[0]
    nk, _, tk = vt.shape
    tq = min(tq, s)
    return pl.pallas_call(
        functools.partial(_mla_kernel, unroll=min(unroll, nk)),
        grid=(MLA_HEADS, s // tq),
        in_specs=[
            pl.BlockSpec((tq, MLA_QK_PAD), lambda h, i: (i, h)),
            pl.BlockSpec((s, MLA_QK_PAD), lambda h, i: (0, h)),
            pl.BlockSpec((nk, MLA_V, tk), lambda h, i: (0, h, 0)),
        ],
        out_specs=pl.BlockSpec((tq, MLA_V), lambda h, i: (i, h)),
        out_shape=jax.ShapeDtypeStruct((s, MLA_WIDTH), F32),
        scratch_shapes=[pltpu.VMEM((MLA_V, tq), F32)],
        compiler_params=_cparams(("parallel", "parallel")),
        name="mla_attention",
    )(q, k, vt)


def _xattn_kernel(q_ref, kv_ref, o_ref, *, d_model, scale):
    dh = d_model // X_HEADS
    for h in range(X_HEADS):
        q = q_ref[:, h * dh:(h + 1) * dh]
        k = kv_ref[:, h * dh:(h + 1) * dh]
        v = kv_ref[:, d_model + h * dh:d_model + (h + 1) * dh]
        s = _nt_dot(q, k) * scale
        m = jnp.max(s, axis=-1, keepdims=True)
        p = jnp.exp(s - m)
        l = jnp.sum(p, axis=-1, keepdims=True)
        o = jnp.dot(p.astype(BF16), v, preferred_element_type=F32) / l
        o_ref[:, h * dh:(h + 1) * dh] = o.astype(o_ref.dtype)


def _xattn(q, kv, tq=512):
    s, d_model = q.shape
    tq = min(tq, s)
    return pl.pallas_call(
        functools.partial(_xattn_kernel, d_model=d_model, scale=(d_model // X_HEADS) ** -0.5),
        grid=(s // tq,),
        in_specs=[
            pl.BlockSpec((tq, d_model), lambda i: (i, 0)),
            pl.BlockSpec(kv.shape, lambda i: (0, 0)),
        ],
        out_specs=pl.BlockSpec((tq, d_model), lambda i: (i, 0)),
        out_shape=jax.ShapeDtypeStruct((s, d_model), BF16),
        compiler_params=_cparams(("parallel",)),
        name="mem_attention",
    )(q, kv)


def _ffn_kernel(x_ref, g_ref, wg_ref, wu_ref, wd_ref, o_ref, xn_ref):
    f = pl.program_id(1)

    @pl.when(f == 0)
    def _():
        x = x_ref[...]
        xn_ref[...] = _rms(x, g_ref[...]).astype(BF16)
        o_ref[...] = x

    xn = xn_ref[...]
    gate = jnp.dot(xn, wg_ref[...], preferred_element_type=F32)
    up = jnp.dot(xn, wu_ref[...], preferred_element_type=F32)
    act = (gate * jax.nn.sigmoid(gate) * up).astype(BF16)
    o_ref[...] += jnp.dot(act, wd_ref[...], preferred_element_type=F32)


def _ffn(x, gain, wg, wu, wd, tm=1024, tf=512):
    s, d = x.shape
    f = wg.shape[1]
    tm = min(tm, s)
    assert s % tm == 0 and f % tf == 0
    return pl.pallas_call(
        _ffn_kernel,
        grid=(s // tm, f // tf),
        in_specs=[
            pl.BlockSpec((tm, d), lambda i, j: (i, 0), pipeline_mode=pl.Buffered(1)),
            pl.BlockSpec((1, d), lambda i, j: (0, 0)),
            pl.BlockSpec((d, tf), lambda i, j: (0, j)),
            pl.BlockSpec((d, tf), lambda i, j: (0, j)),
            pl.BlockSpec((tf, d), lambda i, j: (j, 0)),
        ],
        out_specs=pl.BlockSpec((tm, d), lambda i, j: (i, 0)),
        out_shape=jax.ShapeDtypeStruct((s, d), F32),
        scratch_shapes=[pltpu.VMEM((tm, d), BF16)],
        compiler_params=_cparams(("parallel", "arbitrary")),
        name="ffn",
    )(x, gain.reshape(1, d), wg, wu, wd)


def _norm_kernel(x_ref, g_ref, o_ref):
    o_ref[...] = _rms(x_ref[...], g_ref[...])


def _final_norm(x, gain, tm=512):
    s, d = x.shape
    tm = min(tm, s)
    return pl.pallas_call(
        _norm_kernel,
        grid=(s // tm,),
        in_specs=[pl.BlockSpec((tm, d), lambda i: (i, 0)), pl.BlockSpec((1, d), lambda i: (0, 0))],
        out_specs=pl.BlockSpec((tm, d), lambda i: (i, 0)),
        out_shape=jax.ShapeDtypeStruct((s, d), F32),
        compiler_params=_cparams(("parallel",)),
        name="final_norm",
    )(x, gain.reshape(1, d))


def _split_w_in(w_in):
    d = w_in.shape[0]
    o3 = 3 * NA_WIDTH
    o5 = o3 + MLA_Q_RANK + MLA_KV_RANK
    half = MLA_ROPE // 2
    z = jnp.zeros((d, LANES - MLA_ROPE), w_in.dtype)
    w_lat = jnp.concatenate(
        [w_in[:, o3:], z, w_in[:, o5 + half:], w_in[:, o5:o5 + half], z], axis=1)
    return w_in[:, :o3].astype(BF16), w_lat.astype(BF16)


def _split_w_uq(w_uq):
    r = w_uq.shape[0]
    half = MLA_ROPE // 2
    w = w_uq.reshape(r, MLA_HEADS, MLA_QK)
    wp = jnp.pad(w, ((0, 0), (0, 0), (0, MLA_QK_PAD - MLA_QK)))
    ws = jnp.concatenate(
        [w[:, :, MLA_NOPE + half:], w[:, :, MLA_NOPE:MLA_NOPE + half],
         jnp.zeros((r, MLA_HEADS, LANES - MLA_ROPE), w.dtype)], axis=2)
    return (wp.reshape(r, MLA_HEADS * MLA_QK_PAD).astype(BF16),
            ws.reshape(r, MLA_HEADS * LANES).astype(BF16))


def _split_w_ukv(w_ukv):
    r = w_ukv.shape[0]
    w = w_ukv.reshape(r, MLA_HEADS, MLA_NOPE + MLA_V)
    return (w[:, :, :MLA_NOPE].reshape(r, MLA_HEADS * MLA_NOPE).astype(BF16),
            w[:, :, MLA_NOPE:].reshape(r, MLA_HEADS * MLA_V).T.astype(BF16))


def kernel(x, mem, ln_mix, w_in, q_norm, kv_norm, w_uq, w_ukv, na_rpb, na_out_norm, mla_out_norm,
           w_out, ln_mem, mem_norm, w_xq, w_xk, w_xv, w_xo, ln_ffn, w_gate, w_up, w_down,
           final_norm):
    b, s, d = x.shape
    assert b == 1
    depth = w_in.shape[0]
    x = x.reshape(s, d)
    mem2 = mem.reshape(mem.shape[1], d)
    tab = _rope_tables(s, MLA_SCORE_SCALE)
    for l in range(depth):
        w_qkv, w_lat = _split_w_in(w_in[l])
        wq, wqs = _split_w_uq(w_uq[l])
        wk, wvt = _split_w_ukv(w_ukv[l])
        qkv, c = _nmm([x], [ln_mix[l]], [w_qkv, w_lat], out_dtypes=(BF16, F32), splits=(3, 1))
        y_na = _na_attention(qkv, _na_bias_table(na_rpb[l]))
        ql, kl, vl = _latent_up(c, q_norm[l], kv_norm[l], wq, wqs, wk, wvt, tab,
                                tm=min(MLA_KV_CHUNK, s))
        y_mla = _mla_attention(ql, kl, vl)
        x = _nmm([y_na, y_mla], [na_out_norm[l], mla_out_norm[l]], [w_out[l].astype(BF16)], res=x)
        qx = _nmm([x], [ln_mem[l]], [w_xq[l].astype(BF16)], out_dtypes=(BF16,))
        w_kv = jnp.concatenate([w_xk[l], w_xv[l]], axis=1).astype(BF16)
        kvx = _nmm([mem2], [mem_norm[l]], [w_kv], out_dtypes=(BF16,))
        ox = _xattn(qx, kvx)
        x = _nmm([ox], [None], [w_xo[l].astype(BF16)], res=x)
        x = _ffn(x, ln_ffn[l], w_gate[l].astype(BF16), w_up[l].astype(BF16),
                 w_down[l].astype(BF16))
    return _final_norm(x, final_norm).reshape(b, s, d)
```

```python
import functools

import jax
import jax.numpy as jnp
import numpy as np
from jax import lax
from jax.experimental import pallas as pl
from jax.experimental.pallas import tpu as pltpu

F32 = jnp.float32
BF16 = jnp.bfloat16

EPS = 1e-6
GRID_W = 64
NA_HEADS = 8
NA_HEAD_DIM = 128
NA_WIDTH = NA_HEADS * NA_HEAD_DIM
NA_ROWS = 8
NA_COLS = 16
MLA_HEADS = 8
MLA_Q_RANK = 512
MLA_KV_RANK = 256
MLA_NOPE = 128
MLA_ROPE = 64
MLA_V = 128
MLA_QK = MLA_NOPE + MLA_ROPE
MLA_QK_PAD = 256
MLA_WIDTH = MLA_HEADS * MLA_V
MLA_SCORE_SCALE = MLA_QK ** -0.5 * float(np.log2(np.e))
MLA_KV_CHUNK = 1024
MLA_PANEL = 512
MLA_ROW_BLOCK = 16
MLA_MAX_LAG_GAP = 64.0
ROPE_THETA = 10000.0
X_HEADS = 4
LANES = 128
VMEM_LIMIT = 56 * 1024 * 1024

NA_GROUP_ROWS = 4
NA_WIN_ROWS = NA_GROUP_ROWS + NA_ROWS
NEG_INF = -1e30


def _cparams(sem):
    return pltpu.CompilerParams(dimension_semantics=sem, vmem_limit_bytes=VMEM_LIMIT)


def _nt_dot(a, b):
    return lax.dot_general(a, b, (((1,), (1,)), ((), ())), preferred_element_type=F32)


def _rms(x, g):
    return x * lax.rsqrt(jnp.mean(x * x, axis=-1, keepdims=True) + EPS) * g


def _nmm_kernel(*refs, widths, normed, splits, has_res, sub):
    n = len(widths)
    nw = len(splits)
    x_refs = refs[:n]
    g_refs = refs[n:n + sum(normed)]
    rest = refs[n + sum(normed):]
    w_refs = rest[:nw]
    r_ref = rest[nw] if has_res else None
    o_refs = rest[nw + has_res:]
    tm = x_refs[0].shape[0]
    for r0 in range(0, tm, sub):
        rows = slice(r0, r0 + sub)
        xs = []
        gi = 0
        for p in range(n):
            x = x_refs[p][rows, :]
            if normed[p]:
                x = _rms(x.astype(F32), g_refs[gi][...])
                gi += 1
            xs.append(x.astype(BF16))
        for w_ref, o_ref, split in zip(w_refs, o_refs, splits):
            off = 0
            acc = None
            for p in range(n):
                d = jnp.dot(xs[p], w_ref[off:off + widths[p], :], preferred_element_type=F32)
                acc = d if acc is None else acc + d
                off += widths[p]
            if has_res:
                acc = acc + r_ref[rows, :]
            if split == 1:
                o_ref[rows, :] = acc.astype(o_ref.dtype)
            else:
                wd = acc.shape[1] // split
                for c in range(split):
                    o_ref[c, rows, :] = acc[:, c * wd:(c + 1) * wd].astype(o_ref.dtype)


def _nmm(parts, gains, ws, res=None, out_dtypes=(F32,), splits=(1,), tm=512, sub=256):
    m = parts[0].shape[0]
    widths = tuple(p.shape[1] for p in parts)
    k = sum(widths)
    assert all(w.shape[0] == k for w in ws) and (res is None or len(ws) == 1)
    tm = min(tm, m)
    sub = min(sub, tm)
    assert m % tm == 0 and tm % sub == 0
    normed = tuple(g is not None for g in gains)
    row = lambda i: (i, 0)
    const = lambda i: (0, 0)
    in_specs = [pl.BlockSpec((tm, wd), row) for wd in widths]
    args = list(parts)
    for g, wd in zip(gains, widths):
        if g is not None:
            in_specs.append(pl.BlockSpec((1, wd), const))
            args.append(g.reshape(1, wd).astype(F32))
    for w in ws:
        in_specs.append(pl.BlockSpec(w.shape, const, pipeline_mode=pl.Buffered(1)))
        args.append(w)
    if res is not None:
        in_specs.append(pl.BlockSpec((tm, ws[0].shape[1]), row))
        args.append(res)
    out_specs, out_shapes = [], []
    for w, dt, split in zip(ws, out_dtypes, splits):
        n = w.shape[1]
        if split > 1:
            out_specs.append(pl.BlockSpec((split, tm, n // split), lambda i: (0, i, 0)))
            out_shapes.append(jax.ShapeDtypeStruct((split, m, n // split), dt))
        else:
            out_specs.append(pl.BlockSpec((tm, n), row))
            out_shapes.append(jax.ShapeDtypeStruct((m, n), dt))
    outs = pl.pallas_call(
        functools.partial(_nmm_kernel, widths=widths, normed=normed, splits=tuple(splits),
                          has_res=res is not None, sub=sub),
        grid=(m // tm,),
        in_specs=in_specs,
        out_specs=out_specs,
        out_shape=out_shapes,
        compiler_params=_cparams(("parallel",)),
        name="nmm",
    )(*args)
    return outs if len(outs) > 1 else outs[0]


def _na_kernel(q_ref, k_ref, v_ref, b_ref, o_ref, *, scale):
    for h in range(NA_HEADS):
        hs = slice(h * NA_HEAD_DIM, (h + 1) * NA_HEAD_DIM)
        s = _nt_dot(q_ref[:, hs], k_ref[0, :, hs]) * scale + b_ref[h]
        m = jnp.max(s, axis=-1, keepdims=True)
        p = jnp.exp(s - m)
        l = jnp.sum(p, axis=-1, keepdims=True)
        o = jnp.dot(p.astype(BF16), v_ref[0, :, hs], preferred_element_type=F32)
        o_ref[:, hs] = o / l


def _na_group_layout(rows):
    win0, patterns = [], []
    for g in range(rows // NA_GROUP_ROWS):
        w0 = int(np.clip(g * NA_GROUP_ROWS - NA_ROWS // 2, 0, rows - NA_WIN_ROWS))
        pat = []
        for i in range(NA_GROUP_ROWS):
            r = g * NA_GROUP_ROWS + i
            rs = int(np.clip(r - NA_ROWS // 2, 0, rows - NA_ROWS))
            pat.append((rs - r + NA_ROWS - 1, rs - w0))
        win0.append(w0)
        patterns.append(tuple(pat))
    return win0, patterns


def _na_bias_table(rpb, rows):
    qc = np.arange(GRID_W)
    col_start = np.clip(qc - NA_COLS // 2, 0, GRID_W - NA_COLS)
    kc = np.arange(GRID_W)
    inside = (kc[None, :] >= col_start[:, None]) & (kc[None, :] < col_start[:, None] + NA_COLS)
    dc = np.clip(kc[None, :] - qc[:, None] + (NA_COLS - 1), 0, 2 * NA_COLS - 2)
    full = jnp.where(inside[None, None], rpb[:, :, dc], NEG_INF)
    n_dr = 2 * NA_ROWS - 1
    ext = jnp.concatenate([full, jnp.full_like(full[:, :1], NEG_INF)], axis=1)
    _, patterns = _na_group_layout(rows)
    kinds = [patterns[0], patterns[1], patterns[-1]]
    assert all(p == patterns[1] for p in patterns[1:-1])
    idx = np.full((len(kinds), NA_GROUP_ROWS, NA_WIN_ROWS), n_dr, np.int32)
    for kd, pat in enumerate(kinds):
        for i, (dr0, koff) in enumerate(pat):
            for e in range(NA_ROWS):
                idx[kd, i, koff + e] = dr0 + e
    t = ext[:, idx]
    t = jnp.transpose(t, (1, 0, 2, 4, 3, 5))
    return t.reshape(len(kinds), NA_HEADS, NA_GROUP_ROWS * GRID_W, NA_WIN_ROWS * GRID_W).astype(F32)


def _na_attention(qkv, bias):
    s = qkv.shape[1]
    rows = s // GRID_W
    assert rows % NA_GROUP_ROWS == 0 and rows >= NA_WIN_ROWS + NA_GROUP_ROWS
    groups = rows // NA_GROUP_ROWS
    tq = NA_GROUP_ROWS * GRID_W
    win = NA_WIN_ROWS * GRID_W

    def kv_map(which):
        def index(g):
            r0 = jnp.clip(NA_GROUP_ROWS * g - NA_ROWS // 2, 0, rows - NA_WIN_ROWS)
            return (which, r0 * GRID_W, 0)
        return index

    def bias_map(g):
        kind = jnp.where(g == 0, 0, jnp.where(g == groups - 1, 2, 1))
        return (kind, 0, 0, 0)

    kv_block = (pl.Element(1), pl.Element(win), pl.Element(NA_WIDTH))
    return pl.pallas_call(
        functools.partial(_na_kernel, scale=NA_HEAD_DIM ** -0.5),
        grid=(groups,),
        in_specs=[
            pl.BlockSpec((None, tq, NA_WIDTH), lambda g: (0, g, 0)),
            pl.BlockSpec(kv_block, kv_map(1)),
            pl.BlockSpec(kv_block, kv_map(2)),
            pl.BlockSpec((None,) + bias.shape[1:], bias_map),
        ],
        out_specs=pl.BlockSpec((tq, NA_WIDTH), lambda g: (g, 0)),
        out_shape=jax.ShapeDtypeStruct((s, NA_WIDTH), F32),
        compiler_params=_cparams(("parallel",)),
        name="na_attention",
    )(qkv, qkv, qkv, bias)


def _latup_kernel(c_ref, qg_ref, kvg_ref, wq_ref, wqs_ref, wk_ref, wv_ref, tab_ref,
                  q_ref, k_ref, v_ref, *, scale):
    c = c_ref[...]
    qn = _rms(c[:, :MLA_Q_RANK], qg_ref[...]).astype(BF16)
    kvn = _rms(c[:, MLA_Q_RANK:MLA_Q_RANK + MLA_KV_RANK], kvg_ref[...]).astype(BF16)
    a = jnp.dot(qn, wq_ref[...], preferred_element_type=F32)
    b = jnp.dot(qn, wqs_ref[...], preferred_element_type=F32)
    kn = jnp.dot(kvn, wk_ref[...], preferred_element_type=F32)
    v_ref[...] = _nt_dot(wv_ref[...], kvn).astype(BF16)
    tab = tab_ref[...]
    cq, sq = tab[:, :LANES], tab[:, LANES:2 * LANES]
    ck, sk = tab[:, 2 * LANES:3 * LANES], tab[:, 3 * LANES:]
    r0 = MLA_Q_RANK + MLA_KV_RANK
    kr = (c[:, r0:r0 + LANES] * ck + c[:, r0 + LANES:r0 + 2 * LANES] * sk).astype(BF16)
    for h in range(MLA_HEADS):
        o = h * MLA_QK_PAD
        q_ref[:, o:o + LANES] = (a[:, o:o + LANES] * scale).astype(BF16)
        q_ref[:, o + LANES:o + 2 * LANES] = (
            a[:, o + LANES:o + 2 * LANES] * cq + b[:, h * LANES:(h + 1) * LANES] * sq).astype(BF16)
        k_ref[:, o:o + LANES] = kn[:, h * LANES:(h + 1) * LANES].astype(BF16)
        k_ref[:, o + LANES:o + 2 * LANES] = kr


def _latent_up(c, q_gain, kv_gain, wq, wqs, wk, wvt, tab, tm):
    s = c.shape[0]
    assert s % tm == 0
    const = lambda i: (0, 0)
    row = lambda i: (i, 0)
    return pl.pallas_call(
        functools.partial(_latup_kernel, scale=MLA_SCORE_SCALE),
        grid=(s // tm,),
        in_specs=[
            pl.BlockSpec((tm, c.shape[1]), row),
            pl.BlockSpec((1, MLA_Q_RANK), const),
            pl.BlockSpec((1, MLA_KV_RANK), const),
            pl.BlockSpec(wq.shape, const),
            pl.BlockSpec(wqs.shape, const),
            pl.BlockSpec(wk.shape, const),
            pl.BlockSpec(wvt.shape, const),
            pl.BlockSpec((tm, 4 * LANES), row),
        ],
        out_specs=[
            pl.BlockSpec((tm, MLA_HEADS * MLA_QK_PAD), row),
            pl.BlockSpec((tm, MLA_HEADS * MLA_QK_PAD), row),
            pl.BlockSpec((None, MLA_WIDTH, tm), lambda i: (i, 0, 0)),
        ],
        out_shape=[
            jax.ShapeDtypeStruct((s, MLA_HEADS * MLA_QK_PAD), BF16),
            jax.ShapeDtypeStruct((s, MLA_HEADS * MLA_QK_PAD), BF16),
            jax.ShapeDtypeStruct((s // tm, MLA_WIDTH, tm), BF16),
        ],
        compiler_params=_cparams(("parallel",)),
        name="latent_up",
    )(c, q_gain.reshape(1, -1), kv_gain.reshape(1, -1), wq, wqs, wk, wvt, tab)


def _rope_tables(s, scale):
    half = MLA_ROPE // 2
    inv = 1.0 / (ROPE_THETA ** (jnp.arange(0, MLA_ROPE, 2, dtype=F32) / MLA_ROPE))
    ang = jnp.arange(s, dtype=F32)[:, None] * inv[None, :]
    cos, sin = jnp.cos(ang), jnp.sin(ang)
    z = jnp.zeros((s, LANES - 2 * half), F32)
    ct = jnp.concatenate([cos, cos, z], axis=1)
    st = jnp.concatenate([-sin, sin, z], axis=1)
    return jnp.concatenate([ct * scale, st * scale, ct, st], axis=1)


def _mla_kernel(q_ref, k_ref, vt_ref, o_ref, acc_ref, p_ref):
    nk, _, tk = vt_ref.shape
    tq = q_ref.shape[0]
    q = q_ref[...]

    def scores(j):
        return _nt_dot(k_ref[pl.ds(pl.multiple_of(j * tk, tk), tk), :], q)

    def pv(j, p):
        return jnp.dot(vt_ref[j], p.astype(BF16), preferred_element_type=F32)

    def fold8(v, op):
        parts = [v[i:i + 8] for i in range(0, v.shape[0], 8)]
        while len(parts) > 1:
            parts = [op(parts[i], parts[i + 1]) for i in range(0, len(parts), 2)]
        return parts[0]

    def stream_softmax(s, m_row, p_dst):
        m_blk = jnp.broadcast_to(m_row, (MLA_ROW_BLOCK, s.shape[1]))
        cmax, csum = None, None
        for r in range(0, tk, MLA_ROW_BLOCK):
            v = s[r:r + MLA_ROW_BLOCK]
            e = jnp.exp2(v - m_blk)
            vm, es = fold8(v, jnp.maximum), fold8(e, jnp.add)
            cmax = vm if cmax is None else jnp.maximum(cmax, vm)
            csum = es if csum is None else csum + es
            p_dst[r:r + MLA_ROW_BLOCK, :] = e.astype(BF16)
        return jnp.max(cmax, axis=0, keepdims=True), jnp.sum(csum, axis=0, keepdims=True)

    panels = [slice(c, c + MLA_PANEL) for c in range(0, tq, MLA_PANEL)]
    items = [(j, c) for j in range(nk) for c in range(len(panels))]

    def panel_scores(item):
        j, c = item
        return _nt_dot(k_ref[j * tk:(j + 1) * tk, :], q_ref[panels[c], :])

    m, l, gap = {}, {}, {}
    for w, (j, c) in enumerate(items):
        cs = panels[c]
        s = panel_scores((j, c))
        if j == 0:
            m[c] = jnp.max(fold8(s[:MLA_ROW_BLOCK], jnp.maximum), axis=0, keepdims=True)
        p_dst = p_ref.at[w % 2, :, :]
        chunk_max, chunk_sum = stream_softmax(s, m[c], p_dst)
        part = jnp.dot(vt_ref[j], p_dst[...], preferred_element_type=F32)
        m_new = jnp.maximum(m[c], chunk_max)
        alpha = jnp.exp2(m[c] - m_new)
        if j == 0:
            gap[c] = chunk_max - m[c]
            l[c] = chunk_sum * alpha
            acc_ref[:, cs] = part * alpha
        else:
            gap[c] = jnp.maximum(gap[c], chunk_max - m[c])
            l[c] = (l[c] + chunk_sum) * alpha
            acc_ref[:, cs] = (acc_ref[:, cs] + part) * alpha
        m[c] = m_new
    for c, cs in enumerate(panels):
        o_ref[cs, :] = (acc_ref[:, cs] / l[c]).T
    worst = gap[0]
    for c in range(1, len(panels)):
        worst = jnp.maximum(worst, gap[c])

    @pl.when(jnp.max(worst) > MLA_MAX_LAG_GAP)
    def _():
        acc_ref[...] = jnp.zeros(acc_ref.shape, F32)

        def body(j, carry):
            m_prev, l_prev = carry
            s = scores(j)
            m_new = jnp.maximum(m_prev, jnp.max(s, axis=0, keepdims=True))
            alpha = jnp.exp2(m_prev - m_new)
            p = jnp.exp2(s - m_new)
            acc_ref[...] = alpha * acc_ref[...] + pv(j, p)
            return m_new, alpha * l_prev + jnp.sum(p, axis=0, keepdims=True)

        init = (jnp.full((1, tq), -jnp.inf, F32), jnp.zeros((1, tq), F32))
        _, l_exact = lax.fori_loop(0, nk, body, init)
        o_ref[...] = (acc_ref[...] / l_exact).T


def _mla_attention(q, k, vt, tq=1024):
    s = q.shape[0]
    nk, _, tk = vt.shape
    tq = min(tq, s)
    return pl.pallas_call(
        _mla_kernel,
        grid=(MLA_HEADS, s // tq),
        in_specs=[
            pl.BlockSpec((tq, MLA_QK_PAD), lambda h, i: (i, h)),
            pl.BlockSpec((s, MLA_QK_PAD), lambda h, i: (0, h)),
            pl.BlockSpec((nk, MLA_V, tk), lambda h, i: (0, h, 0)),
        ],
        out_specs=pl.BlockSpec((tq, MLA_V), lambda h, i: (i, h)),
        out_shape=jax.ShapeDtypeStruct((s, MLA_WIDTH), F32),
        scratch_shapes=[pltpu.VMEM((MLA_V, tq), F32), pltpu.VMEM((2, tk, min(MLA_PANEL, tq)), BF16)],
        compiler_params=_cparams(("parallel", "parallel")),
        name="mla_attention",
    )(q, k, vt)


def _xattn_kernel(q_ref, kv_ref, o_ref, *, d_model, scale):
    dh = d_model // X_HEADS
    for h in range(X_HEADS):
        q = q_ref[:, h * dh:(h + 1) * dh]
        k = kv_ref[:, h * dh:(h + 1) * dh]
        v = kv_ref[:, d_model + h * dh:d_model + (h + 1) * dh]
        s = _nt_dot(q, k) * scale
        m = jnp.max(s, axis=-1, keepdims=True)
        p = jnp.exp(s - m)
        l = jnp.sum(p, axis=-1, keepdims=True)
        o = jnp.dot(p.astype(BF16), v, preferred_element_type=F32) / l
        o_ref[:, h * dh:(h + 1) * dh] = o.astype(o_ref.dtype)


def _xattn(q, kv, tq=512):
    s, d_model = q.shape
    tq = min(tq, s)
    return pl.pallas_call(
        functools.partial(_xattn_kernel, d_model=d_model, scale=(d_model // X_HEADS) ** -0.5),
        grid=(s // tq,),
        in_specs=[
            pl.BlockSpec((tq, d_model), lambda i: (i, 0)),
            pl.BlockSpec(kv.shape, lambda i: (0, 0)),
        ],
        out_specs=pl.BlockSpec((tq, d_model), lambda i: (i, 0)),
        out_shape=jax.ShapeDtypeStruct((s, d_model), BF16),
        compiler_params=_cparams(("parallel",)),
        name="mem_attention",
    )(q, kv)


def _ffn_kernel(x_ref, g_ref, wg_ref, wu_ref, wd_ref, o_ref, xn_ref):
    f = pl.program_id(1)

    @pl.when(f == 0)
    def _():
        x = x_ref[...]
        xn_ref[...] = _rms(x, g_ref[...]).astype(BF16)
        o_ref[...] = x

    xn = xn_ref[...]
    gate = jnp.dot(xn, wg_ref[...], preferred_element_type=F32)
    up = jnp.dot(xn, wu_ref[...], preferred_element_type=F32)
    act = (gate * jax.nn.sigmoid(gate) * up).astype(BF16)
    o_ref[...] += jnp.dot(act, wd_ref[...], preferred_element_type=F32)


def _ffn(x, gain, wg, wu, wd, tm=1024, tf=512):
    s, d = x.shape
    f = wg.shape[1]
    tm = min(tm, s)
    assert s % tm == 0 and f % tf == 0
    return pl.pallas_call(
        _ffn_kernel,
        grid=(s // tm, f // tf),
        in_specs=[
            pl.BlockSpec((tm, d), lambda i, j: (i, 0), pipeline_mode=pl.Buffered(1)),
            pl.BlockSpec((1, d), lambda i, j: (0, 0)),
            pl.BlockSpec((d, tf), lambda i, j: (0, j)),
            pl.BlockSpec((d, tf), lambda i, j: (0, j)),
            pl.BlockSpec((tf, d), lambda i, j: (j, 0)),
        ],
        out_specs=pl.BlockSpec((tm, d), lambda i, j: (i, 0)),
        out_shape=jax.ShapeDtypeStruct((s, d), F32),
        scratch_shapes=[pltpu.VMEM((tm, d), BF16)],
        compiler_params=_cparams(("parallel", "arbitrary")),
        name="ffn",
    )(x, gain.reshape(1, d), wg, wu, wd)


def _norm_kernel(x_ref, g_ref, o_ref):
    o_ref[...] = _rms(x_ref[...], g_ref[...])


def _final_norm(x, gain, tm=512):
    s, d = x.shape
    tm = min(tm, s)
    return pl.pallas_call(
        _norm_kernel,
        grid=(s // tm,),
        in_specs=[pl.BlockSpec((tm, d), lambda i: (i, 0)), pl.BlockSpec((1, d), lambda i: (0, 0))],
        out_specs=pl.BlockSpec((tm, d), lambda i: (i, 0)),
        out_shape=jax.ShapeDtypeStruct((s, d), F32),
        compiler_params=_cparams(("parallel",)),
        name="final_norm",
    )(x, gain.reshape(1, d))


def _split_w_in(w_in):
    d = w_in.shape[0]
    o3 = 3 * NA_WIDTH
    o5 = o3 + MLA_Q_RANK + MLA_KV_RANK
    half = MLA_ROPE // 2
    z = jnp.zeros((d, LANES - MLA_ROPE), w_in.dtype)
    w_lat = jnp.concatenate(
        [w_in[:, o3:], z, w_in[:, o5 + half:], w_in[:, o5:o5 + half], z], axis=1)
    return w_in[:, :o3].astype(BF16), w_lat.astype(BF16)


def _split_w_uq(w_uq):
    r = w_uq.shape[0]
    half = MLA_ROPE // 2
    w = w_uq.reshape(r, MLA_HEADS, MLA_QK)
    wp = jnp.pad(w, ((0, 0), (0, 0), (0, MLA_QK_PAD - MLA_QK)))
    ws = jnp.concatenate(
        [w[:, :, MLA_NOPE + half:], w[:, :, MLA_NOPE:MLA_NOPE + half],
         jnp.zeros((r, MLA_HEADS, LANES - MLA_ROPE), w.dtype)], axis=2)
    return (wp.reshape(r, MLA_HEADS * MLA_QK_PAD).astype(BF16),
            ws.reshape(r, MLA_HEADS * LANES).astype(BF16))


def _split_w_ukv(w_ukv):
    r = w_ukv.shape[0]
    w = w_ukv.reshape(r, MLA_HEADS, MLA_NOPE + MLA_V)
    return (w[:, :, :MLA_NOPE].reshape(r, MLA_HEADS * MLA_NOPE).astype(BF16),
            w[:, :, MLA_NOPE:].reshape(r, MLA_HEADS * MLA_V).T.astype(BF16))


def kernel(x, mem, ln_mix, w_in, q_norm, kv_norm, w_uq, w_ukv, na_rpb, na_out_norm, mla_out_norm,
           w_out, ln_mem, mem_norm, w_xq, w_xk, w_xv, w_xo, ln_ffn, w_gate, w_up, w_down,
           final_norm):
    b, s, d = x.shape
    assert b == 1
    depth = w_in.shape[0]
    x = x.reshape(s, d)
    mem2 = mem.reshape(mem.shape[1], d)
    tab = _rope_tables(s, MLA_SCORE_SCALE)
    for l in range(depth):
        w_qkv, w_lat = _split_w_in(w_in[l])
        wq, wqs = _split_w_uq(w_uq[l])
        wk, wvt = _split_w_ukv(w_ukv[l])
        qkv, c = _nmm([x], [ln_mix[l]], [w_qkv, w_lat], out_dtypes=(BF16, F32), splits=(3, 1))
        y_na = _na_attention(qkv, _na_bias_table(na_rpb[l], s // GRID_W))
        ql, kl, vl = _latent_up(c, q_norm[l], kv_norm[l], wq, wqs, wk, wvt, tab,
                                tm=min(MLA_KV_CHUNK, s))
        y_mla = _mla_attention(ql, kl, vl)
        x = _nmm([y_na, y_mla], [na_out_norm[l], mla_out_norm[l]], [w_out[l].astype(BF16)], res=x)
        qx = _nmm([x], [ln_mem[l]], [w_xq[l].astype(BF16)], out_dtypes=(BF16,))
        w_kv = jnp.concatenate([w_xk[l], w_xv[l]], axis=1).astype(BF16)
        kvx = _nmm([mem2], [mem_norm[l]], [w_kv], out_dtypes=(BF16,))
        ox = _xattn(qx, kvx)
        x = _nmm([ox], [None], [w_xo[l].astype(BF16)], res=x)
        x = _ffn(x, ln_ffn[l], w_gate[l].astype(BF16), w_up[l].astype(BF16),
                 w_down[l].astype(BF16))
    return _final_norm(x, final_norm).reshape(b, s, d)
```

```python
import functools

import jax
import jax.numpy as jnp
import numpy as np
from jax import lax
from jax.experimental import pallas as pl
from jax.experimental.pallas import tpu as pltpu

F32 = jnp.float32
BF16 = jnp.bfloat16

EPS = 1e-6
GRID_W = 64
NA_HEADS = 8
NA_HEAD_DIM = 128
NA_WIDTH = NA_HEADS * NA_HEAD_DIM
NA_ROWS = 8
NA_COLS = 16
MLA_HEADS = 8
MLA_Q_RANK = 512
MLA_KV_RANK = 256
MLA_NOPE = 128
MLA_ROPE = 64
MLA_V = 128
MLA_QK = MLA_NOPE + MLA_ROPE
MLA_QK_PAD = 256
MLA_WIDTH = MLA_HEADS * MLA_V
MLA_SCORE_SCALE = MLA_QK ** -0.5 * float(np.log2(np.e))
MLA_KV_CHUNK = 1024
MLA_PANEL = 512
MLA_ROW_BLOCK = 16
MLA_MAX_LAG_GAP = 64.0
ROPE_THETA = 10000.0
X_HEADS = 4
LANES = 128
VMEM_LIMIT = 56 * 1024 * 1024

NA_GROUP_ROWS = 4
NA_WIN_ROWS = NA_GROUP_ROWS + NA_ROWS
NEG_INF = -1e30


def _cparams(sem):
    return pltpu.CompilerParams(dimension_semantics=sem, vmem_limit_bytes=VMEM_LIMIT)


def _nt_dot(a, b):
    return lax.dot_general(a, b, (((1,), (1,)), ((), ())), preferred_element_type=F32)


def _rms(x, g):
    return x * lax.rsqrt(jnp.mean(x * x, axis=-1, keepdims=True) + EPS) * g


def _nmm_kernel(*refs, widths, normed, splits, has_res, sub):
    n = len(widths)
    nw = len(splits)
    x_refs = refs[:n]
    g_refs = refs[n:n + sum(normed)]
    rest = refs[n + sum(normed):]
    w_refs = rest[:nw]
    r_ref = rest[nw] if has_res else None
    o_refs = rest[nw + has_res:]
    tm = x_refs[0].shape[0]
    for r0 in range(0, tm, sub):
        rows = slice(r0, r0 + sub)
        xs = []
        gi = 0
        for p in range(n):
            x = x_refs[p][rows, :]
            if normed[p]:
                x = _rms(x.astype(F32), g_refs[gi][...])
                gi += 1
            xs.append(x.astype(BF16))
        for w_ref, o_ref, split in zip(w_refs, o_refs, splits):
            off = 0
            acc = None
            ncols = o_ref.shape[-1] * split
            for p in range(n):
                d = jnp.dot(xs[p], w_ref[off:off + widths[p], :ncols], preferred_element_type=F32)
                acc = d if acc is None else acc + d
                off += widths[p]
            if has_res:
                acc = acc + r_ref[rows, :]
            if split == 1:
                o_ref[rows, :] = acc.astype(o_ref.dtype)
            else:
                wd = acc.shape[1] // split
                for c in range(split):
                    o_ref[c, rows, :] = acc[:, c * wd:(c + 1) * wd].astype(o_ref.dtype)


def _nmm(parts, gains, ws, res=None, out_dtypes=(F32,), splits=(1,), ncols=None, tm=512,
         sub=256):
    m = parts[0].shape[0]
    widths = tuple(p.shape[1] for p in parts)
    k = sum(widths)
    ncols = [w.shape[1] for w in ws] if ncols is None else ncols
    assert all(w.shape[0] == k for w in ws) and (res is None or len(ws) == 1)
    tm = min(tm, m)
    sub = min(sub, tm)
    assert m % tm == 0 and tm % sub == 0
    normed = tuple(g is not None for g in gains)
    row = lambda i: (i, 0)
    const = lambda i: (0, 0)
    in_specs = [pl.BlockSpec((tm, wd), row) for wd in widths]
    args = list(parts)
    for g, wd in zip(gains, widths):
        if g is not None:
            in_specs.append(pl.BlockSpec((1, wd), const))
            args.append(g.reshape(1, wd).astype(F32))
    for w in ws:
        in_specs.append(pl.BlockSpec(w.shape, const, pipeline_mode=pl.Buffered(1)))
        args.append(w)
    if res is not None:
        in_specs.append(pl.BlockSpec((tm, ncols[0]), row))
        args.append(res)
    out_specs, out_shapes = [], []
    for n, dt, split in zip(ncols, out_dtypes, splits):
        if split > 1:
            out_specs.append(pl.BlockSpec((split, tm, n // split), lambda i: (0, i, 0)))
            out_shapes.append(jax.ShapeDtypeStruct((split, m, n // split), dt))
        else:
            out_specs.append(pl.BlockSpec((tm, n), row))
            out_shapes.append(jax.ShapeDtypeStruct((m, n), dt))
    outs = pl.pallas_call(
        functools.partial(_nmm_kernel, widths=widths, normed=normed, splits=tuple(splits),
                          has_res=res is not None, sub=sub),
        grid=(m // tm,),
        in_specs=in_specs,
        out_specs=out_specs,
        out_shape=out_shapes,
        compiler_params=_cparams(("parallel",)),
        name="nmm",
    )(*args)
    return outs if len(outs) > 1 else outs[0]


def _na_kernel(q_ref, k_ref, v_ref, t_ref, o_ref, b_ref, *, scale, patterns):
    g = pl.program_id(0)
    groups = pl.num_programs(0)

    def build(pattern):
        b_ref[...] = jnp.full(b_ref.shape, NEG_INF, F32)
        for i, (dr0, koff) in enumerate(pattern):
            for e in range(NA_ROWS):
                a = koff + e
                half = slice((a % 2) * GRID_W, (a % 2 + 1) * GRID_W)
                b_ref[:, i * GRID_W:(i + 1) * GRID_W, a * GRID_W:(a + 1) * GRID_W] = (
                    t_ref[:, dr0 + e, :, half])

    first, interior, last = patterns
    pl.when(g == 0)(lambda: build(first))
    pl.when(g == 1)(lambda: build(interior))
    pl.when(g == groups - 1)(lambda: build(last))

    for h in range(NA_HEADS):
        hs = slice(h * NA_HEAD_DIM, (h + 1) * NA_HEAD_DIM)
        s = _nt_dot(q_ref[:, hs], k_ref[0, :, hs]) * scale + b_ref[h]
        m = jnp.max(s, axis=-1, keepdims=True)
        p = jnp.exp(s - m)
        l = jnp.sum(p, axis=-1, keepdims=True)
        o = jnp.dot(p.astype(BF16), v_ref[0, :, hs], preferred_element_type=F32)
        o_ref[:, hs] = o / l


def _na_group_layout(rows):
    win0, patterns = [], []
    for g in range(rows // NA_GROUP_ROWS):
        w0 = int(np.clip(g * NA_GROUP_ROWS - NA_ROWS // 2, 0, rows - NA_WIN_ROWS))
        pat = []
        for i in range(NA_GROUP_ROWS):
            r = g * NA_GROUP_ROWS + i
            rs = int(np.clip(r - NA_ROWS // 2, 0, rows - NA_ROWS))
            pat.append((rs - r + NA_ROWS - 1, rs - w0))
        win0.append(w0)
        patterns.append(tuple(pat))
    return win0, patterns


def _na_bias_rows(rpb):
    qc = np.arange(GRID_W)
    col_start = np.clip(qc - NA_COLS // 2, 0, GRID_W - NA_COLS)
    kc = np.arange(GRID_W)
    inside = (kc[None, :] >= col_start[:, None]) & (kc[None, :] < col_start[:, None] + NA_COLS)
    dc = np.clip(kc[None, :] - qc[:, None] + (NA_COLS - 1), 0, 2 * NA_COLS - 2)
    full = jnp.where(inside, rpb[..., dc], NEG_INF).astype(F32)
    return jnp.concatenate([full, full], axis=-1)


def _na_attention(qkv, bias_rows):
    s = qkv.shape[1]
    rows = s // GRID_W
    assert rows % NA_GROUP_ROWS == 0 and rows >= NA_WIN_ROWS + NA_GROUP_ROWS
    groups = rows // NA_GROUP_ROWS
    tq = NA_GROUP_ROWS * GRID_W
    win = NA_WIN_ROWS * GRID_W
    _, patterns = _na_group_layout(rows)
    assert groups >= 3 and all(p == patterns[1] for p in patterns[1:-1])
    patterns = (patterns[0], patterns[1], patterns[-1])

    def kv_map(which):
        def index(g):
            r0 = jnp.clip(NA_GROUP_ROWS * g - NA_ROWS // 2, 0, rows - NA_WIN_ROWS)
            return (which, r0 * GRID_W, 0)
        return index

    kv_block = (pl.Element(1), pl.Element(win), pl.Element(NA_WIDTH))
    return pl.pallas_call(
        functools.partial(_na_kernel, scale=NA_HEAD_DIM ** -0.5, patterns=patterns),
        grid=(groups,),
        in_specs=[
            pl.BlockSpec((None, tq, NA_WIDTH), lambda g: (0, g, 0)),
            pl.BlockSpec(kv_block, kv_map(1)),
            pl.BlockSpec(kv_block, kv_map(2)),
            pl.BlockSpec(bias_rows.shape, lambda g: (0, 0, 0, 0)),
        ],
        out_specs=pl.BlockSpec((tq, NA_WIDTH), lambda g: (g, 0)),
        out_shape=jax.ShapeDtypeStruct((s, NA_WIDTH), F32),
        scratch_shapes=[pltpu.VMEM((NA_HEADS, tq, win), F32)],
        compiler_params=_cparams(("arbitrary",)),
        name="na_attention",
    )(qkv, qkv, qkv, bias_rows)


def _latup_kernel(c_ref, qg_ref, kvg_ref, wq_ref, wqs_ref, wk_ref, wv_ref, tab_ref,
                  q_ref, k_ref, v_ref, *, scale):
    c = c_ref[...]
    qn = _rms(c[:, :MLA_Q_RANK], qg_ref[...]).astype(BF16)
    kvn = _rms(c[:, MLA_Q_RANK:MLA_Q_RANK + MLA_KV_RANK], kvg_ref[...]).astype(BF16)
    a = jnp.dot(qn, wq_ref[...], preferred_element_type=F32)
    b = jnp.dot(qn, wqs_ref[...], preferred_element_type=F32)
    kn = jnp.dot(kvn, wk_ref[...], preferred_element_type=F32)
    v_ref[...] = _nt_dot(wv_ref[...], kvn).astype(BF16)
    tab = tab_ref[...]
    cq, sq = tab[:, :LANES], tab[:, LANES:2 * LANES]
    ck, sk = tab[:, 2 * LANES:3 * LANES], tab[:, 3 * LANES:]
    r0 = MLA_Q_RANK + MLA_KV_RANK
    kr = (c[:, r0:r0 + LANES] * ck + c[:, r0 + LANES:r0 + 2 * LANES] * sk).astype(BF16)
    for h in range(MLA_HEADS):
        o = h * MLA_QK_PAD
        q_ref[:, o:o + LANES] = (a[:, o:o + LANES] * scale).astype(BF16)
        q_ref[:, o + LANES:o + 2 * LANES] = (
            a[:, o + LANES:o + 2 * LANES] * cq + b[:, h * LANES:(h + 1) * LANES] * sq).astype(BF16)
        k_ref[:, o:o + LANES] = kn[:, h * LANES:(h + 1) * LANES].astype(BF16)
        k_ref[:, o + LANES:o + 2 * LANES] = kr


def _latent_up(c, q_gain, kv_gain, wq, wqs, wk, wvt, tab, tm):
    s = c.shape[0]
    assert s % tm == 0
    const = lambda i: (0, 0)
    row = lambda i: (i, 0)
    return pl.pallas_call(
        functools.partial(_latup_kernel, scale=MLA_SCORE_SCALE),
        grid=(s // tm,),
        in_specs=[
            pl.BlockSpec((tm, c.shape[1]), row),
            pl.BlockSpec((1, MLA_Q_RANK), const),
            pl.BlockSpec((1, MLA_KV_RANK), const),
            pl.BlockSpec(wq.shape, const),
            pl.BlockSpec(wqs.shape, const),
            pl.BlockSpec(wk.shape, const),
            pl.BlockSpec(wvt.shape, const),
            pl.BlockSpec((tm, 4 * LANES), row),
        ],
        out_specs=[
            pl.BlockSpec((tm, MLA_HEADS * MLA_QK_PAD), row),
            pl.BlockSpec((tm, MLA_HEADS * MLA_QK_PAD), row),
            pl.BlockSpec((None, MLA_WIDTH, tm), lambda i: (i, 0, 0)),
        ],
        out_shape=[
            jax.ShapeDtypeStruct((s, MLA_HEADS * MLA_QK_PAD), BF16),
            jax.ShapeDtypeStruct((s, MLA_HEADS * MLA_QK_PAD), BF16),
            jax.ShapeDtypeStruct((s // tm, MLA_WIDTH, tm), BF16),
        ],
        compiler_params=_cparams(("parallel",)),
        name="latent_up",
    )(c, q_gain.reshape(1, -1), kv_gain.reshape(1, -1), wq, wqs, wk, wvt, tab)


def _rope_tables(s, scale):
    half = MLA_ROPE // 2
    inv = 1.0 / (ROPE_THETA ** (jnp.arange(0, MLA_ROPE, 2, dtype=F32) / MLA_ROPE))
    ang = jnp.arange(s, dtype=F32)[:, None] * inv[None, :]
    cos, sin = jnp.cos(ang), jnp.sin(ang)
    z = jnp.zeros((s, LANES - 2 * half), F32)
    ct = jnp.concatenate([cos, cos, z], axis=1)
    st = jnp.concatenate([-sin, sin, z], axis=1)
    return jnp.concatenate([ct * scale, st * scale, ct, st], axis=1)


def _mla_kernel(q_ref, k_ref, vt_ref, o_ref, acc_ref, p_ref):
    nk, _, tk = vt_ref.shape
    tq = q_ref.shape[0]
    q = q_ref[...]

    def scores(j):
        return _nt_dot(k_ref[pl.ds(pl.multiple_of(j * tk, tk), tk), :], q)

    def pv(j, p):
        return jnp.dot(vt_ref[j], p.astype(BF16), preferred_element_type=F32)

    def fold8(v, op):
        parts = [v[i:i + 8] for i in range(0, v.shape[0], 8)]
        while len(parts) > 1:
            parts = [op(parts[i], parts[i + 1]) for i in range(0, len(parts), 2)]
        return parts[0]

    def stream_softmax(s, m_row, p_dst):
        m_blk = jnp.broadcast_to(m_row, (MLA_ROW_BLOCK, s.shape[1]))
        cmax, csum = None, None
        for r in range(0, tk, MLA_ROW_BLOCK):
            v = s[r:r + MLA_ROW_BLOCK]
            e = jnp.exp2(v - m_blk)
            vm, es = fold8(v, jnp.maximum), fold8(e, jnp.add)
            cmax = vm if cmax is None else jnp.maximum(cmax, vm)
            csum = es if csum is None else csum + es
            p_dst[r:r + MLA_ROW_BLOCK, :] = e.astype(BF16)
        return jnp.max(cmax, axis=0, keepdims=True), jnp.sum(csum, axis=0, keepdims=True)

    panels = [slice(c, c + MLA_PANEL) for c in range(0, tq, MLA_PANEL)]
    items = [(j, c) for j in range(nk) for c in range(len(panels))]

    def panel_scores(item):
        j, c = item
        return _nt_dot(k_ref[j * tk:(j + 1) * tk, :], q_ref[panels[c], :])

    m, l, gap = {}, {}, {}
    for w, (j, c) in enumerate(items):
        cs = panels[c]
        s = panel_scores((j, c))
        if j == 0:
            m[c] = jnp.max(fold8(s[:MLA_ROW_BLOCK], jnp.maximum), axis=0, keepdims=True)
        p_dst = p_ref.at[w % 2, :, :]
        chunk_max, chunk_sum = stream_softmax(s, m[c], p_dst)
        part = jnp.dot(vt_ref[j], p_dst[...], preferred_element_type=F32)
        m_new = jnp.maximum(m[c], chunk_max)
        alpha = jnp.exp2(m[c] - m_new)
        if j == 0:
            gap[c] = chunk_max - m[c]
            l[c] = chunk_sum * alpha
            acc_ref[:, cs] = part * alpha
        else:
            gap[c] = jnp.maximum(gap[c], chunk_max - m[c])
            l[c] = (l[c] + chunk_sum) * alpha
            acc_ref[:, cs] = (acc_ref[:, cs] + part) * alpha
        m[c] = m_new
    for c, cs in enumerate(panels):
        o_ref[cs, :] = (acc_ref[:, cs] / l[c]).T
    worst = gap[0]
    for c in range(1, len(panels)):
        worst = jnp.maximum(worst, gap[c])

    @pl.when(jnp.max(worst) > MLA_MAX_LAG_GAP)
    def _():
        acc_ref[...] = jnp.zeros(acc_ref.shape, F32)

        def body(j, carry):
            m_prev, l_prev = carry
            s = scores(j)
            m_new = jnp.maximum(m_prev, jnp.max(s, axis=0, keepdims=True))
            alpha = jnp.exp2(m_prev - m_new)
            p = jnp.exp2(s - m_new)
            acc_ref[...] = alpha * acc_ref[...] + pv(j, p)
            return m_new, alpha * l_prev + jnp.sum(p, axis=0, keepdims=True)

        init = (jnp.full((1, tq), -jnp.inf, F32), jnp.zeros((1, tq), F32))
        _, l_exact = lax.fori_loop(0, nk, body, init)
        o_ref[...] = (acc_ref[...] / l_exact).T


def _mla_attention(q, k, vt, tq=1024):
    s = q.shape[0]
    nk, _, tk = vt.shape
    tq = min(tq, s)
    return pl.pallas_call(
        _mla_kernel,
        grid=(MLA_HEADS, s // tq),
        in_specs=[
            pl.BlockSpec((tq, MLA_QK_PAD), lambda h, i: (i, h)),
            pl.BlockSpec((s, MLA_QK_PAD), lambda h, i: (0, h)),
            pl.BlockSpec((nk, MLA_V, tk), lambda h, i: (0, h, 0)),
        ],
        out_specs=pl.BlockSpec((tq, MLA_V), lambda h, i: (i, h)),
        out_shape=jax.ShapeDtypeStruct((s, MLA_WIDTH), F32),
        scratch_shapes=[pltpu.VMEM((MLA_V, tq), F32), pltpu.VMEM((2, tk, min(MLA_PANEL, tq)), BF16)],
        compiler_params=_cparams(("parallel", "parallel")),
        name="mla_attention",
    )(q, k, vt)


def _xattn_kernel(q_ref, k_ref, v_ref, o_ref, *, d_model, scale):
    dh = d_model // X_HEADS
    for h in range(X_HEADS):
        q = q_ref[:, h * dh:(h + 1) * dh]
        k = k_ref[:, h * dh:(h + 1) * dh]
        v = v_ref[:, h * dh:(h + 1) * dh]
        s = _nt_dot(q, k) * scale
        m = jnp.max(s, axis=-1, keepdims=True)
        p = jnp.exp(s - m)
        l = jnp.sum(p, axis=-1, keepdims=True)
        o = jnp.dot(p.astype(BF16), v, preferred_element_type=F32) / l
        o_ref[:, h * dh:(h + 1) * dh] = o.astype(o_ref.dtype)


def _xattn(q, k, v, tq=512):
    s, d_model = q.shape
    tq = min(tq, s)
    return pl.pallas_call(
        functools.partial(_xattn_kernel, d_model=d_model, scale=(d_model // X_HEADS) ** -0.5),
        grid=(s // tq,),
        in_specs=[
            pl.BlockSpec((tq, d_model), lambda i: (i, 0)),
            pl.BlockSpec(k.shape, lambda i: (0, 0)),
            pl.BlockSpec(v.shape, lambda i: (0, 0)),
        ],
        out_specs=pl.BlockSpec((tq, d_model), lambda i: (i, 0)),
        out_shape=jax.ShapeDtypeStruct((s, d_model), BF16),
        compiler_params=_cparams(("parallel",)),
        name="mem_attention",
    )(q, k, v)


def _ffn_kernel(x_ref, g_ref, wg_ref, wu_ref, wd_ref, o_ref, xn_ref):
    f = pl.program_id(1)

    @pl.when(f == 0)
    def _():
        x = x_ref[...]
        xn_ref[...] = _rms(x, g_ref[...]).astype(BF16)
        o_ref[...] = x

    xn = xn_ref[...]
    gate = jnp.dot(xn, wg_ref[...], preferred_element_type=F32)
    up = jnp.dot(xn, wu_ref[...], preferred_element_type=F32)
    act = (gate * jax.nn.sigmoid(gate) * up).astype(BF16)
    o_ref[...] += jnp.dot(act, wd_ref[...], preferred_element_type=F32)


def _ffn(x, gain, wg, wu, wd, tm=1024, tf=512):
    s, d = x.shape
    f = wg.shape[1]
    tm = min(tm, s)
    assert s % tm == 0 and f % tf == 0
    return pl.pallas_call(
        _ffn_kernel,
        grid=(s // tm, f // tf),
        in_specs=[
            pl.BlockSpec((tm, d), lambda i, j: (i, 0), pipeline_mode=pl.Buffered(1)),
            pl.BlockSpec((1, d), lambda i, j: (0, 0)),
            pl.BlockSpec((d, tf), lambda i, j: (0, j)),
            pl.BlockSpec((d, tf), lambda i, j: (0, j)),
            pl.BlockSpec((tf, d), lambda i, j: (j, 0)),
        ],
        out_specs=pl.BlockSpec((tm, d), lambda i, j: (i, 0)),
        out_shape=jax.ShapeDtypeStruct((s, d), F32),
        scratch_shapes=[pltpu.VMEM((tm, d), BF16)],
        compiler_params=_cparams(("parallel", "arbitrary")),
        name="ffn",
    )(x, gain.reshape(1, d), wg, wu, wd)


def _norm_kernel(x_ref, g_ref, o_ref):
    o_ref[...] = _rms(x_ref[...], g_ref[...])


def _final_norm(x, gain, tm=512):
    s, d = x.shape
    tm = min(tm, s)
    return pl.pallas_call(
        _norm_kernel,
        grid=(s // tm,),
        in_specs=[pl.BlockSpec((tm, d), lambda i: (i, 0)), pl.BlockSpec((1, d), lambda i: (0, 0))],
        out_specs=pl.BlockSpec((tm, d), lambda i: (i, 0)),
        out_shape=jax.ShapeDtypeStruct((s, d), F32),
        compiler_params=_cparams(("parallel",)),
        name="final_norm",
    )(x, gain.reshape(1, d))


def _latent_w_in(w_in):
    o3 = 3 * NA_WIDTH
    o5 = o3 + MLA_Q_RANK + MLA_KV_RANK
    half = MLA_ROPE // 2
    z = jnp.zeros(w_in.shape[:2] + (LANES - MLA_ROPE,), w_in.dtype)
    w_lat = jnp.concatenate(
        [w_in[..., o3:], z, w_in[..., o5 + half:], w_in[..., o5:o5 + half], z], axis=-1)
    return w_lat.astype(BF16)


def _split_w_uq(w_uq):
    lead = w_uq.shape[:2]
    half = MLA_ROPE // 2
    w = w_uq.reshape(lead + (MLA_HEADS, MLA_QK))
    wp = jnp.pad(w, ((0, 0), (0, 0), (0, 0), (0, MLA_QK_PAD - MLA_QK)))
    ws = jnp.concatenate(
        [w[..., MLA_NOPE + half:], w[..., MLA_NOPE:MLA_NOPE + half],
         jnp.zeros(lead + (MLA_HEADS, LANES - MLA_ROPE), w.dtype)], axis=-1)
    return (wp.reshape(lead + (MLA_HEADS * MLA_QK_PAD,)).astype(BF16),
            ws.reshape(lead + (MLA_HEADS * LANES,)).astype(BF16))


def _split_w_ukv(w_ukv):
    lead = w_ukv.shape[:2]
    w = w_ukv.reshape(lead + (MLA_HEADS, MLA_NOPE + MLA_V))
    wk = w[..., :MLA_NOPE].reshape(lead + (MLA_HEADS * MLA_NOPE,))
    wv = w[..., MLA_NOPE:].reshape(lead + (MLA_HEADS * MLA_V,))
    return wk.astype(BF16), jnp.swapaxes(wv, 1, 2).astype(BF16)


def kernel(x, mem, ln_mix, w_in, q_norm, kv_norm, w_uq, w_ukv, na_rpb, na_out_norm, mla_out_norm,
           w_out, ln_mem, mem_norm, w_xq, w_xk, w_xv, w_xo, ln_ffn, w_gate, w_up, w_down,
           final_norm):
    b, s, d = x.shape
    assert b == 1
    depth = w_in.shape[0]
    x = x.reshape(s, d)
    mem2 = mem.reshape(mem.shape[1], d)
    tab = _rope_tables(s, MLA_SCORE_SCALE)
    w_in_b, w_lat = w_in.astype(BF16), _latent_w_in(w_in)
    wq, wqs = _split_w_uq(w_uq)
    wk, wvt = _split_w_ukv(w_ukv)
    bias_rows = _na_bias_rows(na_rpb)
    w_out_b, w_xq_b, w_xk_b, w_xv_b, w_xo_b = (
        w.astype(BF16) for w in (w_out, w_xq, w_xk, w_xv, w_xo))
    w_gate_b, w_up_b, w_down_b = (w.astype(BF16) for w in (w_gate, w_up, w_down))
    for l in range(depth):
        qkv, c = _nmm([x], [ln_mix[l]], [w_in_b[l], w_lat[l]], out_dtypes=(BF16, F32),
                      splits=(3, 1), ncols=[3 * NA_WIDTH, w_lat.shape[-1]])
        y_na = _na_attention(qkv, bias_rows[l])
        ql, kl, vl = _latent_up(c, q_norm[l], kv_norm[l], wq[l], wqs[l], wk[l], wvt[l], tab,
                                tm=min(MLA_KV_CHUNK, s))
        y_mla = _mla_attention(ql, kl, vl)
        x = _nmm([y_na, y_mla], [na_out_norm[l], mla_out_norm[l]], [w_out_b[l]], res=x)
        qx = _nmm([x], [ln_mem[l]], [w_xq_b[l]], out_dtypes=(BF16,))
        kx, vx = _nmm([mem2], [mem_norm[l]], [w_xk_b[l], w_xv_b[l]], out_dtypes=(BF16, BF16),
                      splits=(1, 1))
        ox = _xattn(qx, kx, vx)
        x = _nmm([ox], [None], [w_xo_b[l]], res=x)
        x = _ffn(x, ln_ffn[l], w_gate_b[l], w_up_b[l], w_down_b[l])
    return _final_norm(x, final_norm).reshape(b, s, d)
```

```python
import functools

import jax
import jax.numpy as jnp
import numpy as np
from jax import lax
from jax.experimental import pallas as pl
from jax.experimental.pallas import tpu as pltpu

F32 = jnp.float32
BF16 = jnp.bfloat16

EPS = 1e-6
GRID_W = 64
NA_HEADS = 8
NA_HEAD_DIM = 128
NA_WIDTH = NA_HEADS * NA_HEAD_DIM
NA_ROWS = 8
NA_COLS = 16
MLA_HEADS = 8
MLA_Q_RANK = 512
MLA_KV_RANK = 256
MLA_NOPE = 128
MLA_ROPE = 64
MLA_V = 128
MLA_QK = MLA_NOPE + MLA_ROPE
MLA_QK_PAD = 256
MLA_WIDTH = MLA_HEADS * MLA_V
MLA_SCORE_SCALE = MLA_QK ** -0.5 * float(np.log2(np.e))
MLA_KV_CHUNK = 1024
MLA_PANEL = 512
MLA_ROW_BLOCK = 16
MLA_MAX_LAG_GAP = 64.0
ROPE_THETA = 10000.0
X_HEADS = 4
LANES = 128
VMEM_LIMIT = 56 * 1024 * 1024

NA_GROUP_ROWS = 4
NA_WIN_ROWS = NA_GROUP_ROWS + NA_ROWS
NEG_INF = -1e30


def _cparams(sem):
    return pltpu.CompilerParams(dimension_semantics=sem, vmem_limit_bytes=VMEM_LIMIT)


def _nt_dot(a, b):
    return lax.dot_general(a, b, (((1,), (1,)), ((), ())), preferred_element_type=F32)


def _rms(x, g):
    return x * lax.rsqrt(jnp.mean(x * x, axis=-1, keepdims=True) + EPS) * g


def _nmm_kernel(*refs, widths, normed, splits, has_res, sub):
    n = len(widths)
    nw = len(splits)
    x_refs = refs[:n]
    g_refs = refs[n:n + sum(normed)]
    rest = refs[n + sum(normed):]
    w_refs = rest[:nw]
    r_ref = rest[nw] if has_res else None
    o_refs = rest[nw + has_res:]
    tm = x_refs[0].shape[0]
    for r0 in range(0, tm, sub):
        rows = slice(r0, r0 + sub)
        xs = []
        gi = 0
        for p in range(n):
            x = x_refs[p][rows, :]
            if normed[p]:
                x = _rms(x.astype(F32), g_refs[gi][...])
                gi += 1
            xs.append(x.astype(BF16))
        for w_ref, o_ref, split in zip(w_refs, o_refs, splits):
            off = 0
            acc = None
            ncols = o_ref.shape[-1] * split
            for p in range(n):
                d = jnp.dot(xs[p], w_ref[off:off + widths[p], :ncols], preferred_element_type=F32)
                acc = d if acc is None else acc + d
                off += widths[p]
            if has_res:
                acc = acc + r_ref[rows, :]
            if split == 1:
                o_ref[rows, :] = acc.astype(o_ref.dtype)
            else:
                wd = acc.shape[1] // split
                for c in range(split):
                    o_ref[c, rows, :] = acc[:, c * wd:(c + 1) * wd].astype(o_ref.dtype)


def _nmm(parts, gains, ws, layer, res=None, out_dtypes=(F32,), splits=(1,), ncols=None, tm=512,
         sub=256):
    m = parts[0].shape[0]
    widths = tuple(p.shape[1] for p in parts)
    k = sum(widths)
    ncols = [w.shape[2] for w in ws] if ncols is None else ncols
    assert all(w.shape[1] == k for w in ws) and (res is None or len(ws) == 1)
    tm = min(tm, m)
    sub = min(sub, tm)
    assert m % tm == 0 and tm % sub == 0
    normed = tuple(g is not None for g in gains)
    row = lambda i: (i, 0)
    const = lambda i: (0, 0)
    in_specs = [pl.BlockSpec((tm, wd), row) for wd in widths]
    args = list(parts)
    for g, wd in zip(gains, widths):
        if g is not None:
            in_specs.append(pl.BlockSpec((1, wd), const))
            args.append(g.reshape(1, wd).astype(F32))
    for w in ws:
        in_specs.append(pl.BlockSpec((None,) + w.shape[1:], lambda i: (layer, 0, 0),
                                     pipeline_mode=pl.Buffered(1)))
        args.append(w)
    if res is not None:
        in_specs.append(pl.BlockSpec((tm, ncols[0]), row))
        args.append(res)
    out_specs, out_shapes = [], []
    for n, dt, split in zip(ncols, out_dtypes, splits):
        if split > 1:
            out_specs.append(pl.BlockSpec((split, tm, n // split), lambda i: (0, i, 0)))
            out_shapes.append(jax.ShapeDtypeStruct((split, m, n // split), dt))
        else:
            out_specs.append(pl.BlockSpec((tm, n), row))
            out_shapes.append(jax.ShapeDtypeStruct((m, n), dt))
    outs = pl.pallas_call(
        functools.partial(_nmm_kernel, widths=widths, normed=normed, splits=tuple(splits),
                          has_res=res is not None, sub=sub),
        grid=(m // tm,),
        in_specs=in_specs,
        out_specs=out_specs,
        out_shape=out_shapes,
        compiler_params=_cparams(("parallel",)),
        name="nmm",
    )(*args)
    return outs if len(outs) > 1 else outs[0]


def _na_kernel(q_ref, k_ref, v_ref, t_ref, o_ref, b_ref, *, scale, patterns):
    g = pl.program_id(0)
    groups = pl.num_programs(0)

    def build(pattern):
        b_ref[...] = jnp.full(b_ref.shape, NEG_INF, F32)
        for i, (dr0, koff) in enumerate(pattern):
            for e in range(NA_ROWS):
                a = koff + e
                half = slice((a % 2) * GRID_W, (a % 2 + 1) * GRID_W)
                b_ref[:, i * GRID_W:(i + 1) * GRID_W, a * GRID_W:(a + 1) * GRID_W] = (
                    t_ref[:, dr0 + e, :, half])

    first, interior, last = patterns
    pl.when(g == 0)(lambda: build(first))
    pl.when(g == 1)(lambda: build(interior))
    pl.when(g == groups - 1)(lambda: build(last))

    for h in range(NA_HEADS):
        hs = slice(h * NA_HEAD_DIM, (h + 1) * NA_HEAD_DIM)
        s = _nt_dot(q_ref[:, hs], k_ref[0, :, hs]) * scale + b_ref[h]
        m = jnp.max(s, axis=-1, keepdims=True)
        p = jnp.exp(s - m)
        l = jnp.sum(p, axis=-1, keepdims=True)
        o = jnp.dot(p.astype(BF16), v_ref[0, :, hs], preferred_element_type=F32)
        o_ref[:, hs] = o / l


def _na_group_layout(rows):
    win0, patterns = [], []
    for g in range(rows // NA_GROUP_ROWS):
        w0 = int(np.clip(g * NA_GROUP_ROWS - NA_ROWS // 2, 0, rows - NA_WIN_ROWS))
        pat = []
        for i in range(NA_GROUP_ROWS):
            r = g * NA_GROUP_ROWS + i
            rs = int(np.clip(r - NA_ROWS // 2, 0, rows - NA_ROWS))
            pat.append((rs - r + NA_ROWS - 1, rs - w0))
        win0.append(w0)
        patterns.append(tuple(pat))
    return win0, patterns


def _na_bias_rows(rpb):
    qc = np.arange(GRID_W)
    col_start = np.clip(qc - NA_COLS // 2, 0, GRID_W - NA_COLS)
    kc = np.arange(GRID_W)
    inside = (kc[None, :] >= col_start[:, None]) & (kc[None, :] < col_start[:, None] + NA_COLS)
    dc = np.clip(kc[None, :] - qc[:, None] + (NA_COLS - 1), 0, 2 * NA_COLS - 2)
    full = jnp.where(inside, rpb[..., dc], NEG_INF).astype(F32)
    return jnp.concatenate([full, full], axis=-1)


def _na_attention(qkv, bias_rows):
    s = qkv.shape[1]
    rows = s // GRID_W
    assert rows % NA_GROUP_ROWS == 0 and rows >= NA_WIN_ROWS + NA_GROUP_ROWS
    groups = rows // NA_GROUP_ROWS
    tq = NA_GROUP_ROWS * GRID_W
    win = NA_WIN_ROWS * GRID_W
    _, patterns = _na_group_layout(rows)
    assert groups >= 3 and all(p == patterns[1] for p in patterns[1:-1])
    patterns = (patterns[0], patterns[1], patterns[-1])

    def kv_map(which):
        def index(g):
            r0 = jnp.clip(NA_GROUP_ROWS * g - NA_ROWS // 2, 0, rows - NA_WIN_ROWS)
            return (which, r0 * GRID_W, 0)
        return index

    kv_block = (pl.Element(1), pl.Element(win), pl.Element(NA_WIDTH))
    return pl.pallas_call(
        functools.partial(_na_kernel, scale=NA_HEAD_DIM ** -0.5, patterns=patterns),
        grid=(groups,),
        in_specs=[
            pl.BlockSpec((None, tq, NA_WIDTH), lambda g: (0, g, 0)),
            pl.BlockSpec(kv_block, kv_map(1)),
            pl.BlockSpec(kv_block, kv_map(2)),
            pl.BlockSpec(bias_rows.shape, lambda g: (0, 0, 0, 0)),
        ],
        out_specs=pl.BlockSpec((tq, NA_WIDTH), lambda g: (g, 0)),
        out_shape=jax.ShapeDtypeStruct((s, NA_WIDTH), F32),
        scratch_shapes=[pltpu.VMEM((NA_HEADS, tq, win), F32)],
        compiler_params=_cparams(("arbitrary",)),
        name="na_attention",
    )(qkv, qkv, qkv, bias_rows)


def _latup_kernel(c_ref, qg_ref, kvg_ref, wq_ref, wqs_ref, wk_ref, wv_ref, tab_ref,
                  q_ref, k_ref, v_ref, *, scale):
    c = c_ref[...]
    qn = _rms(c[:, :MLA_Q_RANK], qg_ref[...]).astype(BF16)
    kvn = _rms(c[:, MLA_Q_RANK:MLA_Q_RANK + MLA_KV_RANK], kvg_ref[...]).astype(BF16)
    a = jnp.dot(qn, wq_ref[...], preferred_element_type=F32)
    b = jnp.dot(qn, wqs_ref[...], preferred_element_type=F32)
    kn = jnp.dot(kvn, wk_ref[...], preferred_element_type=F32)
    v_ref[...] = _nt_dot(wv_ref[...], kvn).astype(BF16)
    tab = tab_ref[...]
    cq, sq = tab[:, :LANES], tab[:, LANES:2 * LANES]
    ck, sk = tab[:, 2 * LANES:3 * LANES], tab[:, 3 * LANES:]
    r0 = MLA_Q_RANK + MLA_KV_RANK
    kr = (c[:, r0:r0 + LANES] * ck + c[:, r0 + LANES:r0 + 2 * LANES] * sk).astype(BF16)
    for h in range(MLA_HEADS):
        o = h * MLA_QK_PAD
        q_ref[:, o:o + LANES] = (a[:, o:o + LANES] * scale).astype(BF16)
        q_ref[:, o + LANES:o + 2 * LANES] = (
            a[:, o + LANES:o + 2 * LANES] * cq + b[:, h * LANES:(h + 1) * LANES] * sq).astype(BF16)
        k_ref[:, o:o + LANES] = kn[:, h * LANES:(h + 1) * LANES].astype(BF16)
        k_ref[:, o + LANES:o + 2 * LANES] = kr


def _latent_up(c, q_gain, kv_gain, wq, wqs, wk, wvt, tab, tm):
    s = c.shape[0]
    assert s % tm == 0
    const = lambda i: (0, 0)
    row = lambda i: (i, 0)
    return pl.pallas_call(
        functools.partial(_latup_kernel, scale=MLA_SCORE_SCALE),
        grid=(s // tm,),
        in_specs=[
            pl.BlockSpec((tm, c.shape[1]), row),
            pl.BlockSpec((1, MLA_Q_RANK), const),
            pl.BlockSpec((1, MLA_KV_RANK), const),
            pl.BlockSpec(wq.shape, const),
            pl.BlockSpec(wqs.shape, const),
            pl.BlockSpec(wk.shape, const),
            pl.BlockSpec(wvt.shape, const),
            pl.BlockSpec((tm, 4 * LANES), row),
        ],
        out_specs=[
            pl.BlockSpec((tm, MLA_HEADS * MLA_QK_PAD), row),
            pl.BlockSpec((tm, MLA_HEADS * MLA_QK_PAD), row),
            pl.BlockSpec((None, MLA_WIDTH, tm), lambda i: (i, 0, 0)),
        ],
        out_shape=[
            jax.ShapeDtypeStruct((s, MLA_HEADS * MLA_QK_PAD), BF16),
            jax.ShapeDtypeStruct((s, MLA_HEADS * MLA_QK_PAD), BF16),
            jax.ShapeDtypeStruct((s // tm, MLA_WIDTH, tm), BF16),
        ],
        compiler_params=_cparams(("parallel",)),
        name="latent_up",
    )(c, q_gain.reshape(1, -1), kv_gain.reshape(1, -1), wq, wqs, wk, wvt, tab)


def _rope_tables(s, scale):
    half = MLA_ROPE // 2
    inv = 1.0 / (ROPE_THETA ** (jnp.arange(0, MLA_ROPE, 2, dtype=F32) / MLA_ROPE))
    ang = jnp.arange(s, dtype=F32)[:, None] * inv[None, :]
    cos, sin = jnp.cos(ang), jnp.sin(ang)
    z = jnp.zeros((s, LANES - 2 * half), F32)
    ct = jnp.concatenate([cos, cos, z], axis=1)
    st = jnp.concatenate([-sin, sin, z], axis=1)
    return jnp.concatenate([ct * scale, st * scale, ct, st], axis=1)


def _mla_kernel(q_ref, k_ref, vt_ref, o_ref, acc_ref, p_ref):
    nk, _, tk = vt_ref.shape
    tq = q_ref.shape[0]
    q = q_ref[...]

    def scores(j):
        return _nt_dot(k_ref[pl.ds(pl.multiple_of(j * tk, tk), tk), :], q)

    def pv(j, p):
        return jnp.dot(vt_ref[j], p.astype(BF16), preferred_element_type=F32)

    def fold8(v, op):
        parts = [v[i:i + 8] for i in range(0, v.shape[0], 8)]
        while len(parts) > 1:
            parts = [op(parts[i], parts[i + 1]) for i in range(0, len(parts), 2)]
        return parts[0]

    def stream_softmax(s, m_row, p_dst):
        m_blk = jnp.broadcast_to(m_row, (MLA_ROW_BLOCK, s.shape[1]))
        cmax, csum = None, None
        for r in range(0, tk, MLA_ROW_BLOCK):
            v = s[r:r + MLA_ROW_BLOCK]
            e = jnp.exp2(v - m_blk)
            vm, es = fold8(v, jnp.maximum), fold8(e, jnp.add)
            cmax = vm if cmax is None else jnp.maximum(cmax, vm)
            csum = es if csum is None else csum + es
            p_dst[r:r + MLA_ROW_BLOCK, :] = e.astype(BF16)
        return jnp.max(cmax, axis=0, keepdims=True), jnp.sum(csum, axis=0, keepdims=True)

    panels = [slice(c, c + MLA_PANEL) for c in range(0, tq, MLA_PANEL)]
    items = [(j, c) for j in range(nk) for c in range(len(panels))]

    def panel_scores(item):
        j, c = item
        return _nt_dot(k_ref[j * tk:(j + 1) * tk, :], q_ref[panels[c], :])

    m, l, gap = {}, {}, {}
    for w, (j, c) in enumerate(items):
        cs = panels[c]
        s = panel_scores((j, c))
        if j == 0:
            m[c] = jnp.max(fold8(s[:MLA_ROW_BLOCK], jnp.maximum), axis=0, keepdims=True)
        p_dst = p_ref.at[w % 2, :, :]
        chunk_max, chunk_sum = stream_softmax(s, m[c], p_dst)
        part = jnp.dot(vt_ref[j], p_dst[...], preferred_element_type=F32)
        m_new = jnp.maximum(m[c], chunk_max)
        alpha = jnp.exp2(m[c] - m_new)
        if j == 0:
            gap[c] = chunk_max - m[c]
            l[c] = chunk_sum * alpha
            acc_ref[:, cs] = part * alpha
        else:
            gap[c] = jnp.maximum(gap[c], chunk_max - m[c])
            l[c] = (l[c] + chunk_sum) * alpha
            acc_ref[:, cs] = (acc_ref[:, cs] + part) * alpha
        m[c] = m_new
    for c, cs in enumerate(panels):
        o_ref[cs, :] = (acc_ref[:, cs] / l[c]).T
    worst = gap[0]
    for c in range(1, len(panels)):
        worst = jnp.maximum(worst, gap[c])

    @pl.when(jnp.max(worst) > MLA_MAX_LAG_GAP)
    def _():
        acc_ref[...] = jnp.zeros(acc_ref.shape, F32)

        def body(j, carry):
            m_prev, l_prev = carry
            s = scores(j)
            m_new = jnp.maximum(m_prev, jnp.max(s, axis=0, keepdims=True))
            alpha = jnp.exp2(m_prev - m_new)
            p = jnp.exp2(s - m_new)
            acc_ref[...] = alpha * acc_ref[...] + pv(j, p)
            return m_new, alpha * l_prev + jnp.sum(p, axis=0, keepdims=True)

        init = (jnp.full((1, tq), -jnp.inf, F32), jnp.zeros((1, tq), F32))
        _, l_exact = lax.fori_loop(0, nk, body, init)
        o_ref[...] = (acc_ref[...] / l_exact).T


def _mla_attention(q, k, vt, tq=1024):
    s = q.shape[0]
    nk, _, tk = vt.shape
    tq = min(tq, s)
    return pl.pallas_call(
        _mla_kernel,
        grid=(MLA_HEADS, s // tq),
        in_specs=[
            pl.BlockSpec((tq, MLA_QK_PAD), lambda h, i: (i, h)),
            pl.BlockSpec((s, MLA_QK_PAD), lambda h, i: (0, h)),
            pl.BlockSpec((nk, MLA_V, tk), lambda h, i: (0, h, 0)),
        ],
        out_specs=pl.BlockSpec((tq, MLA_V), lambda h, i: (i, h)),
        out_shape=jax.ShapeDtypeStruct((s, MLA_WIDTH), F32),
        scratch_shapes=[pltpu.VMEM((MLA_V, tq), F32), pltpu.VMEM((2, tk, min(MLA_PANEL, tq)), BF16)],
        compiler_params=_cparams(("parallel", "parallel")),
        name="mla_attention",
    )(q, k, vt)


def _xattn_kernel(q_ref, k_ref, v_ref, o_ref, *, d_model, scale):
    dh = d_model // X_HEADS
    for h in range(X_HEADS):
        q = q_ref[:, h * dh:(h + 1) * dh]
        k = k_ref[:, h * dh:(h + 1) * dh]
        v = v_ref[:, h * dh:(h + 1) * dh]
        s = _nt_dot(q, k) * scale
        m = jnp.max(s, axis=-1, keepdims=True)
        p = jnp.exp(s - m)
        l = jnp.sum(p, axis=-1, keepdims=True)
        o = jnp.dot(p.astype(BF16), v, preferred_element_type=F32) / l
        o_ref[:, h * dh:(h + 1) * dh] = o.astype(o_ref.dtype)


def _xattn(q, k, v, tq=512):
    s, d_model = q.shape
    tq = min(tq, s)
    return pl.pallas_call(
        functools.partial(_xattn_kernel, d_model=d_model, scale=(d_model // X_HEADS) ** -0.5),
        grid=(s // tq,),
        in_specs=[
            pl.BlockSpec((tq, d_model), lambda i: (i, 0)),
            pl.BlockSpec(k.shape, lambda i: (0, 0)),
            pl.BlockSpec(v.shape, lambda i: (0, 0)),
        ],
        out_specs=pl.BlockSpec((tq, d_model), lambda i: (i, 0)),
        out_shape=jax.ShapeDtypeStruct((s, d_model), BF16),
        compiler_params=_cparams(("parallel",)),
        name="mem_attention",
    )(q, k, v)


def _ffn_kernel(x_ref, g_ref, wg_ref, wu_ref, wd_ref, *rest, out_norm):
    og_ref = rest[0] if out_norm else None
    o_ref, xn_ref = rest[-2:]
    f = pl.program_id(1)

    @pl.when(f == 0)
    def _():
        x = x_ref[...]
        xn_ref[...] = _rms(x, g_ref[...]).astype(BF16)
        o_ref[...] = x

    xn = xn_ref[...]
    gate = jnp.dot(xn, wg_ref[...], preferred_element_type=F32)
    up = jnp.dot(xn, wu_ref[...], preferred_element_type=F32)
    act = (gate * jax.nn.sigmoid(gate) * up).astype(BF16)
    o_ref[...] += jnp.dot(act, wd_ref[...], preferred_element_type=F32)

    if out_norm:
        @pl.when(f == pl.num_programs(1) - 1)
        def _():
            o_ref[...] = _rms(o_ref[...], og_ref[...])


def _ffn(x, gain, wg, wu, wd, layer, out_gain=None, tm=1024, tf=512):
    s, d = x.shape
    f = wg.shape[2]
    tm = min(tm, s)
    assert s % tm == 0 and f % tf == 0
    in_specs = [
        pl.BlockSpec((tm, d), lambda i, j: (i, 0), pipeline_mode=pl.Buffered(1)),
        pl.BlockSpec((1, d), lambda i, j: (0, 0)),
        pl.BlockSpec((None, d, tf), lambda i, j: (layer, 0, j)),
        pl.BlockSpec((None, d, tf), lambda i, j: (layer, 0, j)),
        pl.BlockSpec((None, tf, d), lambda i, j: (layer, j, 0)),
    ]
    args = [x, gain.reshape(1, d), wg, wu, wd]
    if out_gain is not None:
        in_specs.append(pl.BlockSpec((1, d), lambda i, j: (0, 0)))
        args.append(out_gain.reshape(1, d))
    return pl.pallas_call(
        functools.partial(_ffn_kernel, out_norm=out_gain is not None),
        grid=(s // tm, f // tf),
        in_specs=in_specs,
        out_specs=pl.BlockSpec((tm, d), lambda i, j: (i, 0)),
        out_shape=jax.ShapeDtypeStruct((s, d), F32),
        scratch_shapes=[pltpu.VMEM((tm, d), BF16)],
        compiler_params=_cparams(("parallel", "arbitrary")),
        name="ffn",
    )(*args)


def _latent_w_in(w_in):
    o3 = 3 * NA_WIDTH
    o5 = o3 + MLA_Q_RANK + MLA_KV_RANK
    half = MLA_ROPE // 2
    z = jnp.zeros(w_in.shape[:2] + (LANES - MLA_ROPE,), w_in.dtype)
    w_lat = jnp.concatenate(
        [w_in[..., o3:], z, w_in[..., o5 + half:], w_in[..., o5:o5 + half], z], axis=-1)
    return w_lat.astype(BF16)


def _split_w_uq(w_uq):
    lead = w_uq.shape[:2]
    half = MLA_ROPE // 2
    w = w_uq.reshape(lead + (MLA_HEADS, MLA_QK))
    wp = jnp.pad(w, ((0, 0), (0, 0), (0, 0), (0, MLA_QK_PAD - MLA_QK)))
    ws = jnp.concatenate(
        [w[..., MLA_NOPE + half:], w[..., MLA_NOPE:MLA_NOPE + half],
         jnp.zeros(lead + (MLA_HEADS, LANES - MLA_ROPE), w.dtype)], axis=-1)
    return (wp.reshape(lead + (MLA_HEADS * MLA_QK_PAD,)).astype(BF16),
            ws.reshape(lead + (MLA_HEADS * LANES,)).astype(BF16))


def _split_w_ukv(w_ukv):
    lead = w_ukv.shape[:2]
    w = w_ukv.reshape(lead + (MLA_HEADS, MLA_NOPE + MLA_V))
    wk = w[..., :MLA_NOPE].reshape(lead + (MLA_HEADS * MLA_NOPE,))
    wv = w[..., MLA_NOPE:].reshape(lead + (MLA_HEADS * MLA_V,))
    return wk.astype(BF16), jnp.swapaxes(wv, 1, 2).astype(BF16)


def kernel(x, mem, ln_mix, w_in, q_norm, kv_norm, w_uq, w_ukv, na_rpb, na_out_norm, mla_out_norm,
           w_out, ln_mem, mem_norm, w_xq, w_xk, w_xv, w_xo, ln_ffn, w_gate, w_up, w_down,
           final_norm):
    b, s, d = x.shape
    assert b == 1
    depth = w_in.shape[0]
    x = x.reshape(s, d)
    mem2 = mem.reshape(mem.shape[1], d)
    tab = _rope_tables(s, MLA_SCORE_SCALE)
    w_in_b, w_lat = w_in.astype(BF16), _latent_w_in(w_in)
    wq, wqs = _split_w_uq(w_uq)
    wk, wvt = _split_w_ukv(w_ukv)
    bias_rows = _na_bias_rows(na_rpb)
    w_out_b, w_xq_b, w_xk_b, w_xv_b, w_xo_b = (
        w.astype(BF16) for w in (w_out, w_xq, w_xk, w_xv, w_xo))
    w_gate_b, w_up_b, w_down_b = (w.astype(BF16) for w in (w_gate, w_up, w_down))
    for l in range(depth):
        qkv, c = _nmm([x], [ln_mix[l]], [w_in_b, w_lat], l, out_dtypes=(BF16, F32),
                      splits=(3, 1), ncols=[3 * NA_WIDTH, w_lat.shape[-1]])
        y_na = _na_attention(qkv, bias_rows[l])
        ql, kl, vl = _latent_up(c, q_norm[l], kv_norm[l], wq[l], wqs[l], wk[l], wvt[l], tab,
                                tm=min(MLA_KV_CHUNK, s))
        y_mla = _mla_attention(ql, kl, vl)
        x = _nmm([y_na, y_mla], [na_out_norm[l], mla_out_norm[l]], [w_out_b], l, res=x)
        qx = _nmm([x], [ln_mem[l]], [w_xq_b], l, out_dtypes=(BF16,))
        kx, vx = _nmm([mem2], [mem_norm[l]], [w_xk_b, w_xv_b], l, out_dtypes=(BF16, BF16),
                      splits=(1, 1))
        ox = _xattn(qx, kx, vx)
        x = _nmm([ox], [None], [w_xo_b], l, res=x)
        x = _ffn(x, ln_ffn[l], w_gate_b, w_up_b, w_down_b, l,
                 out_gain=final_norm if l == depth - 1 else None)
    return x.reshape(b, s, d)
```

```python
import functools

import jax
import jax.numpy as jnp
import numpy as np
from jax import lax
from jax.experimental import pallas as pl
from jax.experimental.pallas import tpu as pltpu

F32 = jnp.float32
BF16 = jnp.bfloat16

EPS = 1e-6
GRID_W = 64
NA_HEADS = 8
NA_HEAD_DIM = 128
NA_WIDTH = NA_HEADS * NA_HEAD_DIM
NA_ROWS = 8
NA_COLS = 16
MLA_HEADS = 8
MLA_Q_RANK = 512
MLA_KV_RANK = 256
MLA_NOPE = 128
MLA_ROPE = 64
MLA_V = 128
MLA_QK = MLA_NOPE + MLA_ROPE
MLA_QK_PAD = 256
MLA_WIDTH = MLA_HEADS * MLA_V
MLA_SCORE_SCALE = MLA_QK ** -0.5 * float(np.log2(np.e))
MLA_KV_CHUNK = 1024
MLA_PANEL = 512
MLA_ROW_BLOCK = 16
MLA_MAX_FAST_SUM = 2.0 ** 100
ROPE_THETA = 10000.0
X_HEADS = 4
LANES = 128
VMEM_LIMIT = 56 * 1024 * 1024

NA_GROUP_ROWS = 4
NA_WIN_ROWS = NA_GROUP_ROWS + NA_ROWS
NEG_INF = -1e30


def _cparams(sem):
    return pltpu.CompilerParams(dimension_semantics=sem, vmem_limit_bytes=VMEM_LIMIT)


def _nt_dot(a, b):
    return lax.dot_general(a, b, (((1,), (1,)), ((), ())), preferred_element_type=F32)


def _rms(x, g):
    return x * lax.rsqrt(jnp.mean(x * x, axis=-1, keepdims=True) + EPS) * g


def _nmm_kernel(*refs, widths, normed, splits, has_res, sub):
    n = len(widths)
    nw = len(splits)
    x_refs = refs[:n]
    g_refs = refs[n:n + sum(normed)]
    rest = refs[n + sum(normed):]
    w_refs = rest[:nw]
    r_ref = rest[nw] if has_res else None
    o_refs = rest[nw + has_res:]
    tm = x_refs[0].shape[0]
    for r0 in range(0, tm, sub):
        rows = slice(r0, r0 + sub)
        xs = []
        gi = 0
        for p in range(n):
            x = x_refs[p][rows, :]
            if normed[p]:
                x = _rms(x.astype(F32), g_refs[gi][...])
                gi += 1
            xs.append(x.astype(BF16))
        for w_ref, o_ref, split in zip(w_refs, o_refs, splits):
            off = 0
            acc = None
            ncols = o_ref.shape[-1] * split
            for p in range(n):
                d = jnp.dot(xs[p], w_ref[off:off + widths[p], :ncols], preferred_element_type=F32)
                acc = d if acc is None else acc + d
                off += widths[p]
            if has_res:
                acc = acc + r_ref[rows, :]
            if split == 1:
                o_ref[rows, :] = acc.astype(o_ref.dtype)
            else:
                wd = acc.shape[1] // split
                for c in range(split):
                    o_ref[c, rows, :] = acc[:, c * wd:(c + 1) * wd].astype(o_ref.dtype)


def _nmm(parts, gains, ws, layer, res=None, out_dtypes=(F32,), splits=(1,), ncols=None, tm=512,
         sub=256):
    m = parts[0].shape[0]
    widths = tuple(p.shape[1] for p in parts)
    k = sum(widths)
    ncols = [w.shape[2] for w in ws] if ncols is None else ncols
    assert all(w.shape[1] == k for w in ws) and (res is None or len(ws) == 1)
    tm = min(tm, m)
    sub = min(sub, tm)
    assert m % tm == 0 and tm % sub == 0
    normed = tuple(g is not None for g in gains)
    row = lambda i: (i, 0)
    const = lambda i: (0, 0)
    in_specs = [pl.BlockSpec((tm, wd), row) for wd in widths]
    args = list(parts)
    for g, wd in zip(gains, widths):
        if g is not None:
            in_specs.append(pl.BlockSpec((1, wd), const))
            args.append(g.reshape(1, wd).astype(F32))
    for w in ws:
        in_specs.append(pl.BlockSpec((None,) + w.shape[1:], lambda i: (layer, 0, 0),
                                     pipeline_mode=pl.Buffered(1)))
        args.append(w)
    if res is not None:
        in_specs.append(pl.BlockSpec((tm, ncols[0]), row))
        args.append(res)
    out_specs, out_shapes = [], []
    for n, dt, split in zip(ncols, out_dtypes, splits):
        if split > 1:
            out_specs.append(pl.BlockSpec((split, tm, n // split), lambda i: (0, i, 0)))
            out_shapes.append(jax.ShapeDtypeStruct((split, m, n // split), dt))
        else:
            out_specs.append(pl.BlockSpec((tm, n), row))
            out_shapes.append(jax.ShapeDtypeStruct((m, n), dt))
    outs = pl.pallas_call(
        functools.partial(_nmm_kernel, widths=widths, normed=normed, splits=tuple(splits),
                          has_res=res is not None, sub=sub),
        grid=(m // tm,),
        in_specs=in_specs,
        out_specs=out_specs,
        out_shape=out_shapes,
        compiler_params=_cparams(("parallel",)),
        name="nmm",
    )(*args)
    return outs if len(outs) > 1 else outs[0]


def _na_kernel(q_ref, k_ref, v_ref, t_ref, o_ref, b_ref, *, scale, patterns):
    g = pl.program_id(0)
    groups = pl.num_programs(0)

    def build(pattern):
        b_ref[...] = jnp.full(b_ref.shape, NEG_INF, F32)
        for i, (dr0, koff) in enumerate(pattern):
            for e in range(NA_ROWS):
                a = koff + e
                half = slice((a % 2) * GRID_W, (a % 2 + 1) * GRID_W)
                b_ref[:, i * GRID_W:(i + 1) * GRID_W, a * GRID_W:(a + 1) * GRID_W] = (
                    t_ref[:, dr0 + e, :, half])

    first, interior, last = patterns
    pl.when(g == 0)(lambda: build(first))
    pl.when(g == 1)(lambda: build(interior))
    pl.when(g == groups - 1)(lambda: build(last))

    for h in range(NA_HEADS):
        hs = slice(h * NA_HEAD_DIM, (h + 1) * NA_HEAD_DIM)
        s = _nt_dot(q_ref[:, hs], k_ref[0, :, hs]) * scale + b_ref[h]
        m = jnp.max(s, axis=-1, keepdims=True)
        p = jnp.exp(s - m)
        l = jnp.sum(p, axis=-1, keepdims=True)
        o = jnp.dot(p.astype(BF16), v_ref[0, :, hs], preferred_element_type=F32)
        o_ref[:, hs] = o / l


def _na_group_layout(rows):
    win0, patterns = [], []
    for g in range(rows // NA_GROUP_ROWS):
        w0 = int(np.clip(g * NA_GROUP_ROWS - NA_ROWS // 2, 0, rows - NA_WIN_ROWS))
        pat = []
        for i in range(NA_GROUP_ROWS):
            r = g * NA_GROUP_ROWS + i
            rs = int(np.clip(r - NA_ROWS // 2, 0, rows - NA_ROWS))
            pat.append((rs - r + NA_ROWS - 1, rs - w0))
        win0.append(w0)
        patterns.append(tuple(pat))
    return win0, patterns


def _na_bias_rows(rpb):
    qc = np.arange(GRID_W)
    col_start = np.clip(qc - NA_COLS // 2, 0, GRID_W - NA_COLS)
    kc = np.arange(GRID_W)
    inside = (kc[None, :] >= col_start[:, None]) & (kc[None, :] < col_start[:, None] + NA_COLS)
    dc = np.clip(kc[None, :] - qc[:, None] + (NA_COLS - 1), 0, 2 * NA_COLS - 2)
    full = jnp.where(inside, rpb[..., dc], NEG_INF).astype(F32)
    return jnp.concatenate([full, full], axis=-1)


def _na_attention(qkv, bias_rows):
    s = qkv.shape[1]
    rows = s // GRID_W
    assert rows % NA_GROUP_ROWS == 0 and rows >= NA_WIN_ROWS + NA_GROUP_ROWS
    groups = rows // NA_GROUP_ROWS
    tq = NA_GROUP_ROWS * GRID_W
    win = NA_WIN_ROWS * GRID_W
    _, patterns = _na_group_layout(rows)
    assert groups >= 3 and all(p == patterns[1] for p in patterns[1:-1])
    patterns = (patterns[0], patterns[1], patterns[-1])

    def kv_map(which):
        def index(g):
            r0 = jnp.clip(NA_GROUP_ROWS * g - NA_ROWS // 2, 0, rows - NA_WIN_ROWS)
            return (which, r0 * GRID_W, 0)
        return index

    kv_block = (pl.Element(1), pl.Element(win), pl.Element(NA_WIDTH))
    return pl.pallas_call(
        functools.partial(_na_kernel, scale=NA_HEAD_DIM ** -0.5, patterns=patterns),
        grid=(groups,),
        in_specs=[
            pl.BlockSpec((None, tq, NA_WIDTH), lambda g: (0, g, 0)),
            pl.BlockSpec(kv_block, kv_map(1)),
            pl.BlockSpec(kv_block, kv_map(2)),
            pl.BlockSpec(bias_rows.shape, lambda g: (0, 0, 0, 0)),
        ],
        out_specs=pl.BlockSpec((tq, NA_WIDTH), lambda g: (g, 0)),
        out_shape=jax.ShapeDtypeStruct((s, NA_WIDTH), F32),
        scratch_shapes=[pltpu.VMEM((NA_HEADS, tq, win), F32)],
        compiler_params=_cparams(("arbitrary",)),
        name="na_attention",
    )(qkv, qkv, qkv, bias_rows)


def _latup_kernel(c_ref, qg_ref, kvg_ref, wq_ref, wqs_ref, wk_ref, wv_ref, tab_ref,
                  q_ref, k_ref, v_ref, *, scale):
    c = c_ref[...]
    qn = _rms(c[:, :MLA_Q_RANK], qg_ref[...]).astype(BF16)
    kvn = _rms(c[:, MLA_Q_RANK:MLA_Q_RANK + MLA_KV_RANK], kvg_ref[...]).astype(BF16)
    a = jnp.dot(qn, wq_ref[...], preferred_element_type=F32)
    b = jnp.dot(qn, wqs_ref[...], preferred_element_type=F32)
    kn = jnp.dot(kvn, wk_ref[...], preferred_element_type=F32)
    v_ref[...] = _nt_dot(wv_ref[...], kvn).astype(BF16)
    tab = tab_ref[...]
    cq, sq = tab[:, :LANES], tab[:, LANES:2 * LANES]
    ck, sk = tab[:, 2 * LANES:3 * LANES], tab[:, 3 * LANES:]
    r0 = MLA_Q_RANK + MLA_KV_RANK
    kr = (c[:, r0:r0 + LANES] * ck + c[:, r0 + LANES:r0 + 2 * LANES] * sk).astype(BF16)
    for h in range(MLA_HEADS):
        o = h * MLA_QK_PAD
        q_ref[:, o:o + LANES] = (a[:, o:o + LANES] * scale).astype(BF16)
        q_ref[:, o + LANES:o + 2 * LANES] = (
            a[:, o + LANES:o + 2 * LANES] * cq + b[:, h * LANES:(h + 1) * LANES] * sq).astype(BF16)
        k_ref[:, o:o + LANES] = kn[:, h * LANES:(h + 1) * LANES].astype(BF16)
        k_ref[:, o + LANES:o + 2 * LANES] = kr


def _latent_up(c, q_gain, kv_gain, wq, wqs, wk, wvt, tab, tm):
    s = c.shape[0]
    assert s % tm == 0
    const = lambda i: (0, 0)
    row = lambda i: (i, 0)
    return pl.pallas_call(
        functools.partial(_latup_kernel, scale=MLA_SCORE_SCALE),
        grid=(s // tm,),
        in_specs=[
            pl.BlockSpec((tm, c.shape[1]), row),
            pl.BlockSpec((1, MLA_Q_RANK), const),
            pl.BlockSpec((1, MLA_KV_RANK), const),
            pl.BlockSpec(wq.shape, const),
            pl.BlockSpec(wqs.shape, const),
            pl.BlockSpec(wk.shape, const),
            pl.BlockSpec(wvt.shape, const),
            pl.BlockSpec((tm, 4 * LANES), row),
        ],
        out_specs=[
            pl.BlockSpec((tm, MLA_HEADS * MLA_QK_PAD), row),
            pl.BlockSpec((tm, MLA_HEADS * MLA_QK_PAD), row),
            pl.BlockSpec((None, MLA_WIDTH, tm), lambda i: (i, 0, 0)),
        ],
        out_shape=[
            jax.ShapeDtypeStruct((s, MLA_HEADS * MLA_QK_PAD), BF16),
            jax.ShapeDtypeStruct((s, MLA_HEADS * MLA_QK_PAD), BF16),
            jax.ShapeDtypeStruct((s // tm, MLA_WIDTH, tm), BF16),
        ],
        compiler_params=_cparams(("parallel",)),
        name="latent_up",
    )(c, q_gain.reshape(1, -1), kv_gain.reshape(1, -1), wq, wqs, wk, wvt, tab)


def _rope_tables(s, scale):
    half = MLA_ROPE // 2
    inv = 1.0 / (ROPE_THETA ** (jnp.arange(0, MLA_ROPE, 2, dtype=F32) / MLA_ROPE))
    ang = jnp.arange(s, dtype=F32)[:, None] * inv[None, :]
    cos, sin = jnp.cos(ang), jnp.sin(ang)
    z = jnp.zeros((s, LANES - 2 * half), F32)
    ct = jnp.concatenate([cos, cos, z], axis=1)
    st = jnp.concatenate([-sin, sin, z], axis=1)
    return jnp.concatenate([ct * scale, st * scale, ct, st], axis=1)


def _mla_kernel(q_ref, k_ref, vt_ref, o_ref, acc_ref, p_ref):
    nk, _, tk = vt_ref.shape
    tq = q_ref.shape[0]
    q = q_ref[...]

    def scores(j):
        return _nt_dot(k_ref[pl.ds(pl.multiple_of(j * tk, tk), tk), :], q)

    def pv(j, p):
        return jnp.dot(vt_ref[j], p.astype(BF16), preferred_element_type=F32)

    def fold8(v, op):
        parts = [v[i:i + 8] for i in range(0, v.shape[0], 8)]
        while len(parts) > 1:
            parts = [op(parts[i], parts[i + 1]) for i in range(0, len(parts), 2)]
        return parts[0]

    def stream_softmax(s, m_row, p_dst):
        m_blk = jnp.broadcast_to(m_row, (MLA_ROW_BLOCK, s.shape[1]))
        csum = None
        for r in range(0, tk, MLA_ROW_BLOCK):
            e = jnp.exp2(s[r:r + MLA_ROW_BLOCK] - m_blk)
            es = fold8(e, jnp.add)
            csum = es if csum is None else csum + es
            p_dst[r:r + MLA_ROW_BLOCK, :] = e.astype(BF16)
        return jnp.sum(csum, axis=0, keepdims=True)

    panels = [slice(c, c + MLA_PANEL) for c in range(0, tq, MLA_PANEL)]
    items = [(j, c) for j in range(nk) for c in range(len(panels))]

    def panel_scores(item):
        j, c = item
        return _nt_dot(k_ref[j * tk:(j + 1) * tk, :], q_ref[panels[c], :])

    m, l = {}, {}
    for w, (j, c) in enumerate(items):
        cs = panels[c]
        s = panel_scores((j, c))
        if j == 0:
            m[c] = jnp.max(fold8(s[:MLA_ROW_BLOCK], jnp.maximum), axis=0, keepdims=True)
        p_dst = p_ref.at[w % 2, :, :]
        chunk_sum = stream_softmax(s, m[c], p_dst)
        part = jnp.dot(vt_ref[j], p_dst[...], preferred_element_type=F32)
        if j == 0:
            l[c] = chunk_sum
            acc_ref[:, cs] = part
        else:
            l[c] = l[c] + chunk_sum
            acc_ref[:, cs] = acc_ref[:, cs] + part
    size = jnp.sum(jnp.abs(acc_ref[...]))
    for c, cs in enumerate(panels):
        o_ref[cs, :] = (acc_ref[:, cs] / l[c]).T
        size = size + jnp.sum(l[c])

    @pl.when(jnp.logical_not(size < MLA_MAX_FAST_SUM))
    def _():
        acc_ref[...] = jnp.zeros(acc_ref.shape, F32)

        def body(j, carry):
            m_prev, l_prev = carry
            s = scores(j)
            m_new = jnp.maximum(m_prev, jnp.max(s, axis=0, keepdims=True))
            alpha = jnp.exp2(m_prev - m_new)
            p = jnp.exp2(s - m_new)
            acc_ref[...] = alpha * acc_ref[...] + pv(j, p)
            return m_new, alpha * l_prev + jnp.sum(p, axis=0, keepdims=True)

        init = (jnp.full((1, tq), -jnp.inf, F32), jnp.zeros((1, tq), F32))
        _, l_exact = lax.fori_loop(0, nk, body, init)
        o_ref[...] = (acc_ref[...] / l_exact).T


def _mla_attention(q, k, vt, tq=1024):
    s = q.shape[0]
    nk, _, tk = vt.shape
    tq = min(tq, s)
    return pl.pallas_call(
        _mla_kernel,
        grid=(MLA_HEADS, s // tq),
        in_specs=[
            pl.BlockSpec((tq, MLA_QK_PAD), lambda h, i: (i, h)),
            pl.BlockSpec((s, MLA_QK_PAD), lambda h, i: (0, h)),
            pl.BlockSpec((nk, MLA_V, tk), lambda h, i: (0, h, 0)),
        ],
        out_specs=pl.BlockSpec((tq, MLA_V), lambda h, i: (i, h)),
        out_shape=jax.ShapeDtypeStruct((s, MLA_WIDTH), F32),
        scratch_shapes=[pltpu.VMEM((MLA_V, tq), F32), pltpu.VMEM((2, tk, min(MLA_PANEL, tq)), BF16)],
        compiler_params=_cparams(("parallel", "parallel")),
        name="mla_attention",
    )(q, k, vt)


def _xattn_kernel(q_ref, k_ref, v_ref, o_ref, *, d_model, scale):
    dh = d_model // X_HEADS
    for h in range(X_HEADS):
        q = q_ref[:, h * dh:(h + 1) * dh]
        k = k_ref[:, h * dh:(h + 1) * dh]
        v = v_ref[:, h * dh:(h + 1) * dh]
        s = _nt_dot(q, k) * scale
        m = jnp.max(s, axis=-1, keepdims=True)
        p = jnp.exp(s - m)
        l = jnp.sum(p, axis=-1, keepdims=True)
        o = jnp.dot(p.astype(BF16), v, preferred_element_type=F32) / l
        o_ref[:, h * dh:(h + 1) * dh] = o.astype(o_ref.dtype)


def _xattn(q, k, v, tq=512):
    s, d_model = q.shape
    tq = min(tq, s)
    return pl.pallas_call(
        functools.partial(_xattn_kernel, d_model=d_model, scale=(d_model // X_HEADS) ** -0.5),
        grid=(s // tq,),
        in_specs=[
            pl.BlockSpec((tq, d_model), lambda i: (i, 0)),
            pl.BlockSpec(k.shape, lambda i: (0, 0)),
            pl.BlockSpec(v.shape, lambda i: (0, 0)),
        ],
        out_specs=pl.BlockSpec((tq, d_model), lambda i: (i, 0)),
        out_shape=jax.ShapeDtypeStruct((s, d_model), BF16),
        compiler_params=_cparams(("parallel",)),
        name="mem_attention",
    )(q, k, v)


def _ffn_kernel(x_ref, g_ref, wg_ref, wu_ref, wd_ref, *rest, out_norm):
    og_ref = rest[0] if out_norm else None
    o_ref, xn_ref = rest[-2:]
    f = pl.program_id(1)

    @pl.when(f == 0)
    def _():
        x = x_ref[...]
        xn_ref[...] = _rms(x, g_ref[...]).astype(BF16)
        o_ref[...] = x

    xn = xn_ref[...]
    gate = jnp.dot(xn, wg_ref[...], preferred_element_type=F32)
    up = jnp.dot(xn, wu_ref[...], preferred_element_type=F32)
    act = (gate * jax.nn.sigmoid(gate) * up).astype(BF16)
    o_ref[...] += jnp.dot(act, wd_ref[...], preferred_element_type=F32)

    if out_norm:
        @pl.when(f == pl.num_programs(1) - 1)
        def _():
            o_ref[...] = _rms(o_ref[...], og_ref[...])


def _ffn(x, gain, wg, wu, wd, layer, out_gain=None, tm=1024, tf=512):
    s, d = x.shape
    f = wg.shape[2]
    tm = min(tm, s)
    assert s % tm == 0 and f % tf == 0
    in_specs = [
        pl.BlockSpec((tm, d), lambda i, j: (i, 0), pipeline_mode=pl.Buffered(1)),
        pl.BlockSpec((1, d), lambda i, j: (0, 0)),
        pl.BlockSpec((None, d, tf), lambda i, j: (layer, 0, j)),
        pl.BlockSpec((None, d, tf), lambda i, j: (layer, 0, j)),
        pl.BlockSpec((None, tf, d), lambda i, j: (layer, j, 0)),
    ]
    args = [x, gain.reshape(1, d), wg, wu, wd]
    if out_gain is not None:
        in_specs.append(pl.BlockSpec((1, d), lambda i, j: (0, 0)))
        args.append(out_gain.reshape(1, d))
    return pl.pallas_call(
        functools.partial(_ffn_kernel, out_norm=out_gain is not None),
        grid=(s // tm, f // tf),
        in_specs=in_specs,
        out_specs=pl.BlockSpec((tm, d), lambda i, j: (i, 0)),
        out_shape=jax.ShapeDtypeStruct((s, d), F32),
        scratch_shapes=[pltpu.VMEM((tm, d), BF16)],
        compiler_params=_cparams(("parallel", "arbitrary")),
        name="ffn",
    )(*args)


def _latent_w_in(w_in):
    o3 = 3 * NA_WIDTH
    o5 = o3 + MLA_Q_RANK + MLA_KV_RANK
    half = MLA_ROPE // 2
    z = jnp.zeros(w_in.shape[:2] + (LANES - MLA_ROPE,), w_in.dtype)
    w_lat = jnp.concatenate(
        [w_in[..., o3:], z, w_in[..., o5 + half:], w_in[..., o5:o5 + half], z], axis=-1)
    return w_lat.astype(BF16)


def _split_w_uq(w_uq):
    lead = w_uq.shape[:2]
    half = MLA_ROPE // 2
    w = w_uq.reshape(lead + (MLA_HEADS, MLA_QK))
    wp = jnp.pad(w, ((0, 0), (0, 0), (0, 0), (0, MLA_QK_PAD - MLA_QK)))
    ws = jnp.concatenate(
        [w[..., MLA_NOPE + half:], w[..., MLA_NOPE:MLA_NOPE + half],
         jnp.zeros(lead + (MLA_HEADS, LANES - MLA_ROPE), w.dtype)], axis=-1)
    return (wp.reshape(lead + (MLA_HEADS * MLA_QK_PAD,)).astype(BF16),
            ws.reshape(lead + (MLA_HEADS * LANES,)).astype(BF16))


def _split_w_ukv(w_ukv):
    lead = w_ukv.shape[:2]
    w = w_ukv.reshape(lead + (MLA_HEADS, MLA_NOPE + MLA_V))
    wk = w[..., :MLA_NOPE].reshape(lead + (MLA_HEADS * MLA_NOPE,))
    wv = w[..., MLA_NOPE:].reshape(lead + (MLA_HEADS * MLA_V,))
    return wk.astype(BF16), jnp.swapaxes(wv, 1, 2).astype(BF16)


def kernel(x, mem, ln_mix, w_in, q_norm, kv_norm, w_uq, w_ukv, na_rpb, na_out_norm, mla_out_norm,
           w_out, ln_mem, mem_norm, w_xq, w_xk, w_xv, w_xo, ln_ffn, w_gate, w_up, w_down,
           final_norm):
    b, s, d = x.shape
    assert b == 1
    depth = w_in.shape[0]
    x = x.reshape(s, d)
    mem2 = mem.reshape(mem.shape[1], d)
    tab = _rope_tables(s, MLA_SCORE_SCALE)
    w_in_b, w_lat = w_in.astype(BF16), _latent_w_in(w_in)
    wq, wqs = _split_w_uq(w_uq)
    wk, wvt = _split_w_ukv(w_ukv)
    bias_rows = _na_bias_rows(na_rpb)
    w_out_b, w_xq_b, w_xk_b, w_xv_b, w_xo_b = (
        w.astype(BF16) for w in (w_out, w_xq, w_xk, w_xv, w_xo))
    w_gate_b, w_up_b, w_down_b = (w.astype(BF16) for w in (w_gate, w_up, w_down))
    for l in range(depth):
        qkv, c = _nmm([x], [ln_mix[l]], [w_in_b, w_lat], l, out_dtypes=(BF16, F32),
                      splits=(3, 1), ncols=[3 * NA_WIDTH, w_lat.shape[-1]])
        y_na = _na_attention(qkv, bias_rows[l])
        ql, kl, vl = _latent_up(c, q_norm[l], kv_norm[l], wq[l], wqs[l], wk[l], wvt[l], tab,
                                tm=min(MLA_KV_CHUNK, s))
        y_mla = _mla_attention(ql, kl, vl)
        x = _nmm([y_na, y_mla], [na_out_norm[l], mla_out_norm[l]], [w_out_b], l, res=x)
        qx = _nmm([x], [ln_mem[l]], [w_xq_b], l, out_dtypes=(BF16,))
        kx, vx = _nmm([mem2], [mem_norm[l]], [w_xk_b, w_xv_b], l, out_dtypes=(BF16, BF16),
                      splits=(1, 1))
        ox = _xattn(qx, kx, vx)
        x = _nmm([ox], [None], [w_xo_b], l, res=x)
        x = _ffn(x, ln_ffn[l], w_gate_b, w_up_b, w_down_b, l,
                 out_gain=final_norm if l == depth - 1 else None)
    return x.reshape(b, s, d)
```

```python
import functools

import jax
import jax.numpy as jnp
import numpy as np
from jax import lax
from jax.experimental import pallas as pl
from jax.experimental.pallas import tpu as pltpu

F32 = jnp.float32
BF16 = jnp.bfloat16

EPS = 1e-6
GRID_W = 64
NA_HEADS = 8
NA_HEAD_DIM = 128
NA_WIDTH = NA_HEADS * NA_HEAD_DIM
NA_ROWS = 8
NA_COLS = 16
MLA_HEADS = 8
MLA_Q_RANK = 512
MLA_KV_RANK = 256
MLA_NOPE = 128
MLA_ROPE = 64
MLA_V = 128
MLA_QK = MLA_NOPE + MLA_ROPE
MLA_QK_PAD = 256
MLA_WIDTH = MLA_HEADS * MLA_V
MLA_SCORE_SCALE = MLA_QK ** -0.5 * float(np.log2(np.e))
MLA_KV_CHUNK = 1024
MLA_PANEL = 512
MLA_ROW_BLOCK = 16
MLA_MAX_FAST_SUM = 2.0 ** 100
ROPE_THETA = 10000.0
X_HEADS = 4
LANES = 128
VMEM_LIMIT = 56 * 1024 * 1024

NA_GROUP_ROWS = 4
NA_WIN_ROWS = NA_GROUP_ROWS + NA_ROWS
NEG_INF = -1e30
NA_LOG2E = float(np.log2(np.e))
NA_SCORE_SCALE = NA_HEAD_DIM ** -0.5 * NA_LOG2E


def _cparams(sem):
    return pltpu.CompilerParams(dimension_semantics=sem, vmem_limit_bytes=VMEM_LIMIT)


def _nt_dot(a, b):
    return lax.dot_general(a, b, (((1,), (1,)), ((), ())), preferred_element_type=F32)


def _rms(x, g):
    return x * lax.rsqrt(jnp.mean(x * x, axis=-1, keepdims=True) + EPS) * g


def _nmm_kernel(*refs, widths, normed, splits, has_res, sub, first_scale):
    n = len(widths)
    nw = len(splits)
    x_refs = refs[:n]
    g_refs = refs[n:n + sum(normed)]
    rest = refs[n + sum(normed):]
    w_refs = rest[:nw]
    r_ref = rest[nw] if has_res else None
    o_refs = rest[nw + has_res:]
    tm = x_refs[0].shape[0]
    for r0 in range(0, tm, sub):
        rows = slice(r0, r0 + sub)
        xs = []
        gi = 0
        for p in range(n):
            x = x_refs[p][rows, :]
            if normed[p]:
                x = _rms(x.astype(F32), g_refs[gi][...])
                gi += 1
            xs.append(x.astype(BF16))
        for w_ref, o_ref, split in zip(w_refs, o_refs, splits):
            off = 0
            acc = None
            ncols = o_ref.shape[-1] * split
            for p in range(n):
                d = jnp.dot(xs[p], w_ref[off:off + widths[p], :ncols], preferred_element_type=F32)
                acc = d if acc is None else acc + d
                off += widths[p]
            if has_res:
                acc = acc + r_ref[rows, :]
            if split == 1:
                o_ref[rows, :] = acc.astype(o_ref.dtype)
            else:
                wd = acc.shape[1] // split
                for c in range(split):
                    piece = acc[:, c * wd:(c + 1) * wd]
                    if c == 0 and first_scale is not None:
                        piece = piece * first_scale
                    o_ref[c, rows, :] = piece.astype(o_ref.dtype)


def _nmm(parts, gains, ws, layer, res=None, out_dtypes=(F32,), splits=(1,), ncols=None,
         first_scale=None, tm=512, sub=256):
    m = parts[0].shape[0]
    widths = tuple(p.shape[1] for p in parts)
    k = sum(widths)
    ncols = [w.shape[2] for w in ws] if ncols is None else ncols
    assert all(w.shape[1] == k for w in ws) and (res is None or len(ws) == 1)
    tm = min(tm, m)
    sub = min(sub, tm)
    assert m % tm == 0 and tm % sub == 0
    normed = tuple(g is not None for g in gains)
    row = lambda i: (i, 0)
    const = lambda i: (0, 0)
    in_specs = [pl.BlockSpec((tm, wd), row) for wd in widths]
    args = list(parts)
    for g, wd in zip(gains, widths):
        if g is not None:
            in_specs.append(pl.BlockSpec((1, wd), const))
            args.append(g.reshape(1, wd).astype(F32))
    for w in ws:
        in_specs.append(pl.BlockSpec((None,) + w.shape[1:], lambda i: (layer, 0, 0),
                                     pipeline_mode=pl.Buffered(1)))
        args.append(w)
    if res is not None:
        in_specs.append(pl.BlockSpec((tm, ncols[0]), row))
        args.append(res)
    out_specs, out_shapes = [], []
    for n, dt, split in zip(ncols, out_dtypes, splits):
        if split > 1:
            out_specs.append(pl.BlockSpec((split, tm, n // split), lambda i: (0, i, 0)))
            out_shapes.append(jax.ShapeDtypeStruct((split, m, n // split), dt))
        else:
            out_specs.append(pl.BlockSpec((tm, n), row))
            out_shapes.append(jax.ShapeDtypeStruct((m, n), dt))
    outs = pl.pallas_call(
        functools.partial(_nmm_kernel, widths=widths, normed=normed, splits=tuple(splits),
                          has_res=res is not None, sub=sub, first_scale=first_scale),
        grid=(m // tm,),
        in_specs=in_specs,
        out_specs=out_specs,
        out_shape=out_shapes,
        compiler_params=_cparams(("parallel",)),
        name="nmm",
    )(*args)
    return outs if len(outs) > 1 else outs[0]


def _na_kernel(q_ref, k_ref, v_ref, t_ref, o_ref, b_ref, *, patterns):
    g = pl.program_id(0)
    groups = pl.num_programs(0)

    def build(pattern):
        b_ref[...] = jnp.full(b_ref.shape, NEG_INF, F32)
        for i, (dr0, koff) in enumerate(pattern):
            for e in range(NA_ROWS):
                a = koff + e
                half = slice((a % 2) * GRID_W, (a % 2 + 1) * GRID_W)
                b_ref[:, i * GRID_W:(i + 1) * GRID_W, a * GRID_W:(a + 1) * GRID_W] = (
                    t_ref[:, dr0 + e, :, half])

    first, interior, last = patterns
    pl.when(g == 0)(lambda: build(first))
    pl.when(g == 1)(lambda: build(interior))
    pl.when(g == groups - 1)(lambda: build(last))

    for h in range(NA_HEADS):
        hs = slice(h * NA_HEAD_DIM, (h + 1) * NA_HEAD_DIM)
        s = _nt_dot(q_ref[:, hs], k_ref[0, :, hs]) + b_ref[h]
        m = jnp.max(s, axis=-1, keepdims=True)
        p = jnp.exp2(s - m)
        l = jnp.sum(p, axis=-1, keepdims=True)
        o = jnp.dot(p.astype(BF16), v_ref[0, :, hs], preferred_element_type=F32)
        o_ref[:, hs] = o / l


def _na_group_layout(rows):
    win0, patterns = [], []
    for g in range(rows // NA_GROUP_ROWS):
        w0 = int(np.clip(g * NA_GROUP_ROWS - NA_ROWS // 2, 0, rows - NA_WIN_ROWS))
        pat = []
        for i in range(NA_GROUP_ROWS):
            r = g * NA_GROUP_ROWS + i
            rs = int(np.clip(r - NA_ROWS // 2, 0, rows - NA_ROWS))
            pat.append((rs - r + NA_ROWS - 1, rs - w0))
        win0.append(w0)
        patterns.append(tuple(pat))
    return win0, patterns


def _na_bias_rows(rpb):
    qc = np.arange(GRID_W)
    col_start = np.clip(qc - NA_COLS // 2, 0, GRID_W - NA_COLS)
    kc = np.arange(GRID_W)
    inside = (kc[None, :] >= col_start[:, None]) & (kc[None, :] < col_start[:, None] + NA_COLS)
    period = 2 * GRID_W
    lead = rpb.shape[:-1]
    u = jnp.concatenate(
        [rpb[..., NA_COLS - 1:], jnp.zeros(lead + (period - 2 * NA_COLS + 1,), rpb.dtype),
         rpb[..., :NA_COLS - 1]], axis=-1)
    flat = jnp.tile(u, GRID_W)[..., :GRID_W * (period - 1)]
    toeplitz = flat.reshape(lead + (GRID_W, period - 1))[..., :GRID_W]
    full = jnp.where(inside, toeplitz * NA_LOG2E, NEG_INF).astype(F32)
    return jnp.concatenate([full, full], axis=-1)


def _na_attention(qkv, bias_rows):
    s = qkv.shape[1]
    rows = s // GRID_W
    assert rows % NA_GROUP_ROWS == 0 and rows >= NA_WIN_ROWS + NA_GROUP_ROWS
    groups = rows // NA_GROUP_ROWS
    tq = NA_GROUP_ROWS * GRID_W
    win = NA_WIN_ROWS * GRID_W
    _, patterns = _na_group_layout(rows)
    assert groups >= 3 and all(p == patterns[1] for p in patterns[1:-1])
    patterns = (patterns[0], patterns[1], patterns[-1])

    def kv_map(which):
        def index(g):
            r0 = jnp.clip(NA_GROUP_ROWS * g - NA_ROWS // 2, 0, rows - NA_WIN_ROWS)
            return (which, r0 * GRID_W, 0)
        return index

    kv_block = (pl.Element(1), pl.Element(win), pl.Element(NA_WIDTH))
    return pl.pallas_call(
        functools.partial(_na_kernel, patterns=patterns),
        grid=(groups,),
        in_specs=[
            pl.BlockSpec((None, tq, NA_WIDTH), lambda g: (0, g, 0)),
            pl.BlockSpec(kv_block, kv_map(1)),
            pl.BlockSpec(kv_block, kv_map(2)),
            pl.BlockSpec(bias_rows.shape, lambda g: (0, 0, 0, 0)),
        ],
        out_specs=pl.BlockSpec((tq, NA_WIDTH), lambda g: (g, 0)),
        out_shape=jax.ShapeDtypeStruct((s, NA_WIDTH), F32),
        scratch_shapes=[pltpu.VMEM((NA_HEADS, tq, win), F32)],
        compiler_params=_cparams(("arbitrary",)),
        name="na_attention",
    )(qkv, qkv, qkv, bias_rows)


def _latup_kernel(c_ref, qg_ref, kvg_ref, wq_ref, wqs_ref, wk_ref, wv_ref, tab_ref,
                  q_ref, k_ref, v_ref, *, scale):
    c = c_ref[...]
    qn = _rms(c[:, :MLA_Q_RANK], qg_ref[...]).astype(BF16)
    kvn = _rms(c[:, MLA_Q_RANK:MLA_Q_RANK + MLA_KV_RANK], kvg_ref[...]).astype(BF16)
    a = jnp.dot(qn, wq_ref[...], preferred_element_type=F32)
    b = jnp.dot(qn, wqs_ref[...], preferred_element_type=F32)
    kn = jnp.dot(kvn, wk_ref[...], preferred_element_type=F32)
    v_ref[...] = _nt_dot(wv_ref[...], kvn).astype(BF16)
    tab = tab_ref[...]
    cq, sq = tab[:, :LANES], tab[:, LANES:2 * LANES]
    ck, sk = tab[:, 2 * LANES:3 * LANES], tab[:, 3 * LANES:]
    r0 = MLA_Q_RANK + MLA_KV_RANK
    kr = (c[:, r0:r0 + LANES] * ck + c[:, r0 + LANES:r0 + 2 * LANES] * sk).astype(BF16)
    for h in range(MLA_HEADS):
        o = h * MLA_QK_PAD
        q_ref[:, o:o + LANES] = (a[:, o:o + LANES] * scale).astype(BF16)
        q_ref[:, o + LANES:o + 2 * LANES] = (
            a[:, o + LANES:o + 2 * LANES] * cq + b[:, h * LANES:(h + 1) * LANES] * sq).astype(BF16)
        k_ref[:, o:o + LANES] = kn[:, h * LANES:(h + 1) * LANES].astype(BF16)
        k_ref[:, o + LANES:o + 2 * LANES] = kr


def _latent_up(c, q_gain, kv_gain, wq, wqs, wk, wvt, tab, tm):
    s = c.shape[0]
    assert s % tm == 0
    const = lambda i: (0, 0)
    row = lambda i: (i, 0)
    return pl.pallas_call(
        functools.partial(_latup_kernel, scale=MLA_SCORE_SCALE),
        grid=(s // tm,),
        in_specs=[
            pl.BlockSpec((tm, c.shape[1]), row),
            pl.BlockSpec((1, MLA_Q_RANK), const),
            pl.BlockSpec((1, MLA_KV_RANK), const),
            pl.BlockSpec(wq.shape, const),
            pl.BlockSpec(wqs.shape, const),
            pl.BlockSpec(wk.shape, const),
            pl.BlockSpec(wvt.shape, const),
            pl.BlockSpec((tm, 4 * LANES), row),
        ],
        out_specs=[
            pl.BlockSpec((tm, MLA_HEADS * MLA_QK_PAD), row),
            pl.BlockSpec((tm, MLA_HEADS * MLA_QK_PAD), row),
            pl.BlockSpec((None, MLA_WIDTH, tm), lambda i: (i, 0, 0)),
        ],
        out_shape=[
            jax.ShapeDtypeStruct((s, MLA_HEADS * MLA_QK_PAD), BF16),
            jax.ShapeDtypeStruct((s, MLA_HEADS * MLA_QK_PAD), BF16),
            jax.ShapeDtypeStruct((s // tm, MLA_WIDTH, tm), BF16),
        ],
        compiler_params=_cparams(("parallel",)),
        name="latent_up",
    )(c, q_gain.reshape(1, -1), kv_gain.reshape(1, -1), wq, wqs, wk, wvt, tab)


def _rope_tables(s, scale):
    half = MLA_ROPE // 2
    inv = 1.0 / (ROPE_THETA ** (jnp.arange(0, MLA_ROPE, 2, dtype=F32) / MLA_ROPE))
    ang = jnp.arange(s, dtype=F32)[:, None] * inv[None, :]
    cos, sin = jnp.cos(ang), jnp.sin(ang)
    z = jnp.zeros((s, LANES - 2 * half), F32)
    ct = jnp.concatenate([cos, cos, z], axis=1)
    st = jnp.concatenate([-sin, sin, z], axis=1)
    return jnp.concatenate([ct * scale, st * scale, ct, st], axis=1)


def _mla_kernel(q_ref, k_ref, vt_ref, o_ref, acc_ref, p_ref):
    nk, _, tk = vt_ref.shape
    tq = q_ref.shape[0]
    q = q_ref[...]

    def scores(j):
        return _nt_dot(k_ref[pl.ds(pl.multiple_of(j * tk, tk), tk), :], q)

    def pv(j, p):
        return jnp.dot(vt_ref[j], p.astype(BF16), preferred_element_type=F32)

    def fold8(v, op):
        parts = [v[i:i + 8] for i in range(0, v.shape[0], 8)]
        while len(parts) > 1:
            parts = [op(parts[i], parts[i + 1]) for i in range(0, len(parts), 2)]
        return parts[0]

    def stream_softmax(s, m_row, p_dst):
        m_blk = jnp.broadcast_to(m_row, (MLA_ROW_BLOCK, s.shape[1]))
        csum = None
        for r in range(0, tk, MLA_ROW_BLOCK):
            e = jnp.exp2(s[r:r + MLA_ROW_BLOCK] - m_blk)
            es = fold8(e, jnp.add)
            csum = es if csum is None else csum + es
            p_dst[r:r + MLA_ROW_BLOCK, :] = e.astype(BF16)
        return jnp.sum(csum, axis=0, keepdims=True)

    panels = [slice(c, c + MLA_PANEL) for c in range(0, tq, MLA_PANEL)]
    items = [(j, c) for j in range(nk) for c in range(len(panels))]

    def panel_scores(item):
        j, c = item
        return _nt_dot(k_ref[j * tk:(j + 1) * tk, :], q_ref[panels[c], :])

    m, l = {}, {}
    for w, (j, c) in enumerate(items):
        cs = panels[c]
        s = panel_scores((j, c))
        if j == 0:
            m[c] = jnp.max(fold8(s[:MLA_ROW_BLOCK], jnp.maximum), axis=0, keepdims=True)
        p_dst = p_ref.at[w % 2, :, :]
        chunk_sum = stream_softmax(s, m[c], p_dst)
        part = jnp.dot(vt_ref[j], p_dst[...], preferred_element_type=F32)
        if j == 0:
            l[c] = chunk_sum
            acc_ref[:, cs] = part
        else:
            l[c] = l[c] + chunk_sum
            acc_ref[:, cs] = acc_ref[:, cs] + part
    size = jnp.sum(jnp.abs(acc_ref[...]))
    for c, cs in enumerate(panels):
        o_ref[cs, :] = (acc_ref[:, cs] / l[c]).T
        size = size + jnp.sum(l[c])

    @pl.when(jnp.logical_not(size < MLA_MAX_FAST_SUM))
    def _():
        acc_ref[...] = jnp.zeros(acc_ref.shape, F32)

        def body(j, carry):
            m_prev, l_prev = carry
            s = scores(j)
            m_new = jnp.maximum(m_prev, jnp.max(s, axis=0, keepdims=True))
            alpha = jnp.exp2(m_prev - m_new)
            p = jnp.exp2(s - m_new)
            acc_ref[...] = alpha * acc_ref[...] + pv(j, p)
            return m_new, alpha * l_prev + jnp.sum(p, axis=0, keepdims=True)

        init = (jnp.full((1, tq), -jnp.inf, F32), jnp.zeros((1, tq), F32))
        _, l_exact = lax.fori_loop(0, nk, body, init)
        o_ref[...] = (acc_ref[...] / l_exact).T


def _mla_attention(q, k, vt, tq=1024):
    s = q.shape[0]
    nk, _, tk = vt.shape
    tq = min(tq, s)
    return pl.pallas_call(
        _mla_kernel,
        grid=(MLA_HEADS, s // tq),
        in_specs=[
            pl.BlockSpec((tq, MLA_QK_PAD), lambda h, i: (i, h)),
            pl.BlockSpec((s, MLA_QK_PAD), lambda h, i: (0, h)),
            pl.BlockSpec((nk, MLA_V, tk), lambda h, i: (0, h, 0)),
        ],
        out_specs=pl.BlockSpec((tq, MLA_V), lambda h, i: (i, h)),
        out_shape=jax.ShapeDtypeStruct((s, MLA_WIDTH), F32),
        scratch_shapes=[pltpu.VMEM((MLA_V, tq), F32), pltpu.VMEM((2, tk, min(MLA_PANEL, tq)), BF16)],
        compiler_params=_cparams(("parallel", "parallel")),
        name="mla_attention",
    )(q, k, vt)


def _xattn_kernel(q_ref, k_ref, v_ref, o_ref, *, d_model, scale):
    dh = d_model // X_HEADS
    for h in range(X_HEADS):
        q = q_ref[:, h * dh:(h + 1) * dh]
        k = k_ref[:, h * dh:(h + 1) * dh]
        v = v_ref[:, h * dh:(h + 1) * dh]
        s = _nt_dot(q, k) * scale
        m = jnp.max(s, axis=-1, keepdims=True)
        p = jnp.exp(s - m)
        l = jnp.sum(p, axis=-1, keepdims=True)
        o = jnp.dot(p.astype(BF16), v, preferred_element_type=F32) / l
        o_ref[:, h * dh:(h + 1) * dh] = o.astype(o_ref.dtype)


def _xattn(q, k, v, tq=512):
    s, d_model = q.shape
    tq = min(tq, s)
    return pl.pallas_call(
        functools.partial(_xattn_kernel, d_model=d_model, scale=(d_model // X_HEADS) ** -0.5),
        grid=(s // tq,),
        in_specs=[
            pl.BlockSpec((tq, d_model), lambda i: (i, 0)),
            pl.BlockSpec(k.shape, lambda i: (0, 0)),
            pl.BlockSpec(v.shape, lambda i: (0, 0)),
        ],
        out_specs=pl.BlockSpec((tq, d_model), lambda i: (i, 0)),
        out_shape=jax.ShapeDtypeStruct((s, d_model), BF16),
        compiler_params=_cparams(("parallel",)),
        name="mem_attention",
    )(q, k, v)


def _ffn_kernel(x_ref, g_ref, wg_ref, wu_ref, wd_ref, *rest, out_norm):
    og_ref = rest[0] if out_norm else None
    o_ref, xn_ref = rest[-2:]
    f = pl.program_id(1)

    @pl.when(f == 0)
    def _():
        x = x_ref[...]
        xn_ref[...] = _rms(x, g_ref[...]).astype(BF16)
        o_ref[...] = x

    xn = xn_ref[...]
    gate = jnp.dot(xn, wg_ref[...], preferred_element_type=F32)
    up = jnp.dot(xn, wu_ref[...], preferred_element_type=F32)
    act = (gate * jax.nn.sigmoid(gate) * up).astype(BF16)
    o_ref[...] += jnp.dot(act, wd_ref[...], preferred_element_type=F32)

    if out_norm:
        @pl.when(f == pl.num_programs(1) - 1)
        def _():
            o_ref[...] = _rms(o_ref[...], og_ref[...])


def _ffn(x, gain, wg, wu, wd, layer, out_gain=None, tm=1024, tf=512):
    s, d = x.shape
    f = wg.shape[2]
    tm = min(tm, s)
    assert s % tm == 0 and f % tf == 0
    in_specs = [
        pl.BlockSpec((tm, d), lambda i, j: (i, 0), pipeline_mode=pl.Buffered(1)),
        pl.BlockSpec((1, d), lambda i, j: (0, 0)),
        pl.BlockSpec((None, d, tf), lambda i, j: (layer, 0, j)),
        pl.BlockSpec((None, d, tf), lambda i, j: (layer, 0, j)),
        pl.BlockSpec((None, tf, d), lambda i, j: (layer, j, 0)),
    ]
    args = [x, gain.reshape(1, d), wg, wu, wd]
    if out_gain is not None:
        in_specs.append(pl.BlockSpec((1, d), lambda i, j: (0, 0)))
        args.append(out_gain.reshape(1, d))
    return pl.pallas_call(
        functools.partial(_ffn_kernel, out_norm=out_gain is not None),
        grid=(s // tm, f // tf),
        in_specs=in_specs,
        out_specs=pl.BlockSpec((tm, d), lambda i, j: (i, 0)),
        out_shape=jax.ShapeDtypeStruct((s, d), F32),
        scratch_shapes=[pltpu.VMEM((tm, d), BF16)],
        compiler_params=_cparams(("parallel", "arbitrary")),
        name="ffn",
    )(*args)


def _latent_w_in(w_in):
    o3 = 3 * NA_WIDTH
    o5 = o3 + MLA_Q_RANK + MLA_KV_RANK
    half = MLA_ROPE // 2
    z = jnp.zeros(w_in.shape[:2] + (LANES - MLA_ROPE,), w_in.dtype)
    w_lat = jnp.concatenate(
        [w_in[..., o3:], z, w_in[..., o5 + half:], w_in[..., o5:o5 + half], z], axis=-1)
    return w_lat.astype(BF16)


def _split_w_uq(w_uq):
    lead = w_uq.shape[:2]
    half = MLA_ROPE // 2
    w = w_uq.reshape(lead + (MLA_HEADS, MLA_QK))
    wp = jnp.pad(w, ((0, 0), (0, 0), (0, 0), (0, MLA_QK_PAD - MLA_QK)))
    ws = jnp.concatenate(
        [w[..., MLA_NOPE + half:], w[..., MLA_NOPE:MLA_NOPE + half],
         jnp.zeros(lead + (MLA_HEADS, LANES - MLA_ROPE), w.dtype)], axis=-1)
    return (wp.reshape(lead + (MLA_HEADS * MLA_QK_PAD,)).astype(BF16),
            ws.reshape(lead + (MLA_HEADS * LANES,)).astype(BF16))


def _split_w_ukv(w_ukv):
    lead = w_ukv.shape[:2]
    w = w_ukv.reshape(lead + (MLA_HEADS, MLA_NOPE + MLA_V))
    wk = w[..., :MLA_NOPE].reshape(lead + (MLA_HEADS * MLA_NOPE,))
    wv = w[..., MLA_NOPE:].reshape(lead + (MLA_HEADS * MLA_V,))
    return wk.astype(BF16), jnp.swapaxes(wv, 1, 2).astype(BF16)


def kernel(x, mem, ln_mix, w_in, q_norm, kv_norm, w_uq, w_ukv, na_rpb, na_out_norm, mla_out_norm,
           w_out, ln_mem, mem_norm, w_xq, w_xk, w_xv, w_xo, ln_ffn, w_gate, w_up, w_down,
           final_norm):
    b, s, d = x.shape
    assert b == 1
    depth = w_in.shape[0]
    x = x.reshape(s, d)
    mem2 = mem.reshape(mem.shape[1], d)
    tab = _rope_tables(s, MLA_SCORE_SCALE)
    w_in_b, w_lat = w_in.astype(BF16), _latent_w_in(w_in)
    wq, wqs = _split_w_uq(w_uq)
    wk, wvt = _split_w_ukv(w_ukv)
    bias_rows = _na_bias_rows(na_rpb)
    w_out_b, w_xq_b, w_xk_b, w_xv_b, w_xo_b = (
        w.astype(BF16) for w in (w_out, w_xq, w_xk, w_xv, w_xo))
    w_gate_b, w_up_b, w_down_b = (w.astype(BF16) for w in (w_gate, w_up, w_down))
    for l in range(depth):
        qkv, c = _nmm([x], [ln_mix[l]], [w_in_b, w_lat], l, out_dtypes=(BF16, F32),
                      splits=(3, 1), ncols=[3 * NA_WIDTH, w_lat.shape[-1]],
                      first_scale=NA_SCORE_SCALE)
        y_na = _na_attention(qkv, bias_rows[l])
        ql, kl, vl = _latent_up(c, q_norm[l], kv_norm[l], wq[l], wqs[l], wk[l], wvt[l], tab,
                                tm=min(MLA_KV_CHUNK, s))
        y_mla = _mla_attention(ql, kl, vl)
        x = _nmm([y_na, y_mla], [na_out_norm[l], mla_out_norm[l]], [w_out_b], l, res=x)
        qx = _nmm([x], [ln_mem[l]], [w_xq_b], l, out_dtypes=(BF16,))
        kx, vx = _nmm([mem2], [mem_norm[l]], [w_xk_b, w_xv_b], l, out_dtypes=(BF16, BF16),
                      splits=(1, 1))
        ox = _xattn(qx, kx, vx)
        x = _nmm([ox], [None], [w_xo_b], l, res=x)
        x = _ffn(x, ln_ffn[l], w_gate_b, w_up_b, w_down_b, l,
                 out_gain=final_norm if l == depth - 1 else None)
    return x.reshape(b, s, d)
```

```python
import functools

import jax
import jax.numpy as jnp
import numpy as np
from jax import lax
from jax.experimental import pallas as pl
from jax.experimental.pallas import tpu as pltpu

F32 = jnp.float32
BF16 = jnp.bfloat16

EPS = 1e-6
GRID_W = 64
NA_HEADS = 8
NA_HEAD_DIM = 128
NA_WIDTH = NA_HEADS * NA_HEAD_DIM
NA_ROWS = 8
NA_COLS = 16
MLA_HEADS = 8
MLA_Q_RANK = 512
MLA_KV_RANK = 256
MLA_NOPE = 128
MLA_ROPE = 64
MLA_V = 128
MLA_QK = MLA_NOPE + MLA_ROPE
MLA_QK_PAD = 256
MLA_WIDTH = MLA_HEADS * MLA_V
MLA_SCORE_SCALE = MLA_QK ** -0.5 * float(np.log2(np.e))
MLA_KV_CHUNK = 1024
MLA_PANEL = 512
MLA_ROW_BLOCK = 16
MLA_MAX_FAST_SUM = 2.0 ** 100
ROPE_THETA = 10000.0
X_HEADS = 4
LANES = 128
VMEM_LIMIT = 56 * 1024 * 1024

NA_GROUP_ROWS = 4
NA_WIN_ROWS = NA_GROUP_ROWS + NA_ROWS
NEG_INF = -1e30
NA_LOG2E = float(np.log2(np.e))
NA_SCORE_SCALE = NA_HEAD_DIM ** -0.5 * NA_LOG2E


def _cparams(sem):
    return pltpu.CompilerParams(dimension_semantics=sem, vmem_limit_bytes=VMEM_LIMIT)


def _nt_dot(a, b):
    return lax.dot_general(a, b, (((1,), (1,)), ((), ())), preferred_element_type=F32)


def _rms(x, g):
    return x * lax.rsqrt(jnp.mean(x * x, axis=-1, keepdims=True) + EPS) * g


def _nmm_kernel(*refs, widths, normed, splits, has_res, sub, first_scale):
    n = len(widths)
    nw = len(splits)
    x_refs = refs[:n]
    g_refs = refs[n:n + sum(normed)]
    rest = refs[n + sum(normed):]
    w_refs = rest[:nw]
    r_ref = rest[nw] if has_res else None
    o_refs = rest[nw + has_res:]
    tm = x_refs[0].shape[0]
    for r0 in range(0, tm, sub):
        rows = slice(r0, r0 + sub)
        xs = []
        gi = 0
        for p in range(n):
            x = x_refs[p][rows, :]
            if normed[p]:
                x = _rms(x.astype(F32), g_refs[gi][...])
                gi += 1
            xs.append(x.astype(BF16))
        for w_ref, o_ref, split in zip(w_refs, o_refs, splits):
            off = 0
            acc = None
            ncols = o_ref.shape[-1] * split
            for p in range(n):
                d = jnp.dot(xs[p], w_ref[off:off + widths[p], :ncols], preferred_element_type=F32)
                acc = d if acc is None else acc + d
                off += widths[p]
            if has_res:
                acc = acc + r_ref[rows, :]
            if split == 1:
                o_ref[rows, :] = acc.astype(o_ref.dtype)
            else:
                wd = acc.shape[1] // split
                for c in range(split):
                    piece = acc[:, c * wd:(c + 1) * wd]
                    if c == 0 and first_scale is not None:
                        piece = piece * first_scale
                    o_ref[c, rows, :] = piece.astype(o_ref.dtype)


def _nmm(parts, gains, ws, layer, res=None, out_dtypes=(F32,), splits=(1,), ncols=None,
         first_scale=None, tm=512, sub=256):
    m = parts[0].shape[0]
    widths = tuple(p.shape[1] for p in parts)
    k = sum(widths)
    ncols = [w.shape[2] for w in ws] if ncols is None else ncols
    assert all(w.shape[1] == k for w in ws) and (res is None or len(ws) == 1)
    tm = min(tm, m)
    sub = min(sub, tm)
    assert m % tm == 0 and tm % sub == 0
    normed = tuple(g is not None for g in gains)
    row = lambda i: (i, 0)
    const = lambda i: (0, 0)
    in_specs = [pl.BlockSpec((tm, wd), row) for wd in widths]
    args = list(parts)
    for g, wd in zip(gains, widths):
        if g is not None:
            in_specs.append(pl.BlockSpec((1, wd), const))
            args.append(g.reshape(1, wd).astype(F32))
    for w in ws:
        in_specs.append(pl.BlockSpec((None,) + w.shape[1:], lambda i: (layer, 0, 0),
                                     pipeline_mode=pl.Buffered(1)))
        args.append(w)
    if res is not None:
        in_specs.append(pl.BlockSpec((tm, ncols[0]), row))
        args.append(res)
    out_specs, out_shapes = [], []
    for n, dt, split in zip(ncols, out_dtypes, splits):
        if split > 1:
            out_specs.append(pl.BlockSpec((split, tm, n // split), lambda i: (0, i, 0)))
            out_shapes.append(jax.ShapeDtypeStruct((split, m, n // split), dt))
        else:
            out_specs.append(pl.BlockSpec((tm, n), row))
            out_shapes.append(jax.ShapeDtypeStruct((m, n), dt))
    outs = pl.pallas_call(
        functools.partial(_nmm_kernel, widths=widths, normed=normed, splits=tuple(splits),
                          has_res=res is not None, sub=sub, first_scale=first_scale),
        grid=(m // tm,),
        in_specs=in_specs,
        out_specs=out_specs,
        out_shape=out_shapes,
        compiler_params=_cparams(("parallel",)),
        name="nmm",
    )(*args)
    return outs if len(outs) > 1 else outs[0]


def _na_kernel(q_ref, k_ref, v_ref, t_ref, o_ref, b_ref, *, patterns):
    g = pl.program_id(0)
    groups = pl.num_programs(0)

    def build(pattern):
        b_ref[...] = jnp.full(b_ref.shape, NEG_INF, F32)
        for i, (dr0, koff) in enumerate(pattern):
            for e in range(NA_ROWS):
                a = koff + e
                half = slice((a % 2) * GRID_W, (a % 2 + 1) * GRID_W)
                b_ref[:, i * GRID_W:(i + 1) * GRID_W, a * GRID_W:(a + 1) * GRID_W] = (
                    t_ref[:, dr0 + e, :, half])

    first, interior, last = patterns
    pl.when(g == 0)(lambda: build(first))
    pl.when(g == 1)(lambda: build(interior))
    pl.when(g == groups - 1)(lambda: build(last))

    for h in range(NA_HEADS):
        hs = slice(h * NA_HEAD_DIM, (h + 1) * NA_HEAD_DIM)
        s = _nt_dot(q_ref[:, hs], k_ref[0, :, hs]) + b_ref[h]
        m = jnp.max(s, axis=-1, keepdims=True)
        p = jnp.exp2(s - m)
        l = jnp.sum(p, axis=-1, keepdims=True)
        o = jnp.dot(p.astype(BF16), v_ref[0, :, hs], preferred_element_type=F32)
        o_ref[:, hs] = o / l


def _na_group_layout(rows):
    win0, patterns = [], []
    for g in range(rows // NA_GROUP_ROWS):
        w0 = int(np.clip(g * NA_GROUP_ROWS - NA_ROWS // 2, 0, rows - NA_WIN_ROWS))
        pat = []
        for i in range(NA_GROUP_ROWS):
            r = g * NA_GROUP_ROWS + i
            rs = int(np.clip(r - NA_ROWS // 2, 0, rows - NA_ROWS))
            pat.append((rs - r + NA_ROWS - 1, rs - w0))
        win0.append(w0)
        patterns.append(tuple(pat))
    return win0, patterns


def _na_bias_rows(rpb):
    qc = np.arange(GRID_W)
    col_start = np.clip(qc - NA_COLS // 2, 0, GRID_W - NA_COLS)
    kc = np.arange(GRID_W)
    inside = (kc[None, :] >= col_start[:, None]) & (kc[None, :] < col_start[:, None] + NA_COLS)
    period = 2 * GRID_W
    lead = rpb.shape[:-1]
    u = jnp.concatenate(
        [rpb[..., NA_COLS - 1:], jnp.zeros(lead + (period - 2 * NA_COLS + 1,), rpb.dtype),
         rpb[..., :NA_COLS - 1]], axis=-1)
    flat = jnp.tile(u, GRID_W)[..., :GRID_W * (period - 1)]
    toeplitz = flat.reshape(lead + (GRID_W, period - 1))[..., :GRID_W]
    full = jnp.where(inside, toeplitz * NA_LOG2E, NEG_INF).astype(F32)
    return jnp.concatenate([full, full], axis=-1)


def _na_attention(qkv, bias_rows):
    s = qkv.shape[1]
    rows = s // GRID_W
    assert rows % NA_GROUP_ROWS == 0 and rows >= NA_WIN_ROWS + NA_GROUP_ROWS
    groups = rows // NA_GROUP_ROWS
    tq = NA_GROUP_ROWS * GRID_W
    win = NA_WIN_ROWS * GRID_W
    _, patterns = _na_group_layout(rows)
    assert groups >= 3 and all(p == patterns[1] for p in patterns[1:-1])
    patterns = (patterns[0], patterns[1], patterns[-1])

    def kv_map(which):
        def index(g):
            r0 = jnp.clip(NA_GROUP_ROWS * g - NA_ROWS // 2, 0, rows - NA_WIN_ROWS)
            return (which, r0 * GRID_W, 0)
        return index

    kv_block = (pl.Element(1), pl.Element(win), pl.Element(NA_WIDTH))
    return pl.pallas_call(
        functools.partial(_na_kernel, patterns=patterns),
        grid=(groups,),
        in_specs=[
            pl.BlockSpec((None, tq, NA_WIDTH), lambda g: (0, g, 0)),
            pl.BlockSpec(kv_block, kv_map(1)),
            pl.BlockSpec(kv_block, kv_map(2)),
            pl.BlockSpec(bias_rows.shape, lambda g: (0, 0, 0, 0)),
        ],
        out_specs=pl.BlockSpec((tq, NA_WIDTH), lambda g: (g, 0)),
        out_shape=jax.ShapeDtypeStruct((s, NA_WIDTH), F32),
        scratch_shapes=[pltpu.VMEM((NA_HEADS, tq, win), F32)],
        compiler_params=_cparams(("arbitrary",)),
        name="na_attention",
    )(qkv, qkv, qkv, bias_rows)


def _latup_kernel(c_ref, qg_ref, kvg_ref, wq_ref, wqs_ref, wk_ref, wv_ref, tab_ref,
                  q_ref, k_ref, v_ref, *, scale):
    c = c_ref[...]
    qn = _rms(c[:, :MLA_Q_RANK], qg_ref[...]).astype(BF16)
    kvn = _rms(c[:, MLA_Q_RANK:MLA_Q_RANK + MLA_KV_RANK], kvg_ref[...]).astype(BF16)
    a = jnp.dot(qn, wq_ref[...], preferred_element_type=F32)
    b = jnp.dot(qn, wqs_ref[...], preferred_element_type=F32)
    kn = jnp.dot(kvn, wk_ref[...], preferred_element_type=F32)
    v_ref[...] = _nt_dot(wv_ref[...], kvn).astype(BF16)
    tab = tab_ref[...]
    cq, sq = tab[:, :LANES], tab[:, LANES:2 * LANES]
    ck, sk = tab[:, 2 * LANES:3 * LANES], tab[:, 3 * LANES:]
    r0 = MLA_Q_RANK + MLA_KV_RANK
    kr = (c[:, r0:r0 + LANES] * ck + c[:, r0 + LANES:r0 + 2 * LANES] * sk).astype(BF16)
    for h in range(MLA_HEADS):
        o = h * MLA_QK_PAD
        q_ref[:, o:o + LANES] = (a[:, o:o + LANES] * scale).astype(BF16)
        q_ref[:, o + LANES:o + 2 * LANES] = (
            a[:, o + LANES:o + 2 * LANES] * cq + b[:, h * LANES:(h + 1) * LANES] * sq).astype(BF16)
        k_ref[:, o:o + LANES] = kn[:, h * LANES:(h + 1) * LANES].astype(BF16)
        k_ref[:, o + LANES:o + 2 * LANES] = kr


def _latent_up(c, q_gain, kv_gain, wq, wqs, wk, wvt, tab, tm):
    s = c.shape[0]
    assert s % tm == 0
    const = lambda i: (0, 0)
    row = lambda i: (i, 0)
    return pl.pallas_call(
        functools.partial(_latup_kernel, scale=MLA_SCORE_SCALE),
        grid=(s // tm,),
        in_specs=[
            pl.BlockSpec((tm, c.shape[1]), row),
            pl.BlockSpec((1, MLA_Q_RANK), const),
            pl.BlockSpec((1, MLA_KV_RANK), const),
            pl.BlockSpec(wq.shape, const),
            pl.BlockSpec(wqs.shape, const),
            pl.BlockSpec(wk.shape, const),
            pl.BlockSpec(wvt.shape, const),
            pl.BlockSpec((tm, 4 * LANES), row),
        ],
        out_specs=[
            pl.BlockSpec((tm, MLA_HEADS * MLA_QK_PAD), row),
            pl.BlockSpec((tm, MLA_HEADS * MLA_QK_PAD), row),
            pl.BlockSpec((None, MLA_WIDTH, tm), lambda i: (i, 0, 0)),
        ],
        out_shape=[
            jax.ShapeDtypeStruct((s, MLA_HEADS * MLA_QK_PAD), BF16),
            jax.ShapeDtypeStruct((s, MLA_HEADS * MLA_QK_PAD), BF16),
            jax.ShapeDtypeStruct((s // tm, MLA_WIDTH, tm), BF16),
        ],
        compiler_params=_cparams(("parallel",)),
        name="latent_up",
    )(c, q_gain.reshape(1, -1), kv_gain.reshape(1, -1), wq, wqs, wk, wvt, tab)


def _rope_tables(s, scale):
    half = MLA_ROPE // 2
    inv = 1.0 / (ROPE_THETA ** (jnp.arange(0, MLA_ROPE, 2, dtype=F32) / MLA_ROPE))
    ang = jnp.arange(s, dtype=F32)[:, None] * inv[None, :]
    cos, sin = jnp.cos(ang), jnp.sin(ang)
    z = jnp.zeros((s, LANES - 2 * half), F32)
    ct = jnp.concatenate([cos, cos, z], axis=1)
    st = jnp.concatenate([-sin, sin, z], axis=1)
    return jnp.concatenate([ct * scale, st * scale, ct, st], axis=1)


def _mla_kernel(q_ref, k_ref, vt_ref, o_ref, acc_ref, p_ref):
    nk, _, tk = vt_ref.shape
    tq = q_ref.shape[0]
    q = q_ref[...]

    def scores(j):
        return _nt_dot(k_ref[pl.ds(pl.multiple_of(j * tk, tk), tk), :], q)

    def pv(j, p):
        return jnp.dot(vt_ref[j], p.astype(BF16), preferred_element_type=F32)

    def fold8(v, op):
        parts = [v[i:i + 8] for i in range(0, v.shape[0], 8)]
        while len(parts) > 1:
            parts = [op(parts[i], parts[i + 1]) for i in range(0, len(parts), 2)]
        return parts[0]

    def stream_softmax(s, m_row, p_dst):
        m_blk = jnp.broadcast_to(m_row, (MLA_ROW_BLOCK, s.shape[1]))
        csum = None
        for r in range(0, tk, MLA_ROW_BLOCK):
            e = jnp.exp2(s[r:r + MLA_ROW_BLOCK] - m_blk)
            es = fold8(e, jnp.add)
            csum = es if csum is None else csum + es
            p_dst[r:r + MLA_ROW_BLOCK, :] = e.astype(BF16)
        return jnp.sum(csum, axis=0, keepdims=True)

    panels = [slice(c, c + MLA_PANEL) for c in range(0, tq, MLA_PANEL)]
    items = [(j, c) for j in range(nk) for c in range(len(panels))]

    def panel_scores(item):
        j, c = item
        return _nt_dot(k_ref[j * tk:(j + 1) * tk, :], q_ref[panels[c], :])

    m, l = {}, {}
    for w, (j, c) in enumerate(items):
        cs = panels[c]
        s = panel_scores((j, c))
        if j == 0:
            m[c] = jnp.max(fold8(s[:MLA_ROW_BLOCK], jnp.maximum), axis=0, keepdims=True)
        p_dst = p_ref.at[w % 2, :, :]
        chunk_sum = stream_softmax(s, m[c], p_dst)
        part = jnp.dot(vt_ref[j], p_dst[...], preferred_element_type=F32)
        if j == 0:
            l[c] = chunk_sum
            acc_ref[:, cs] = part
        else:
            l[c] = l[c] + chunk_sum
            acc_ref[:, cs] = acc_ref[:, cs] + part
    size = jnp.sum(jnp.abs(acc_ref[...]))
    for c, cs in enumerate(panels):
        o_ref[cs, :] = (acc_ref[:, cs] / l[c]).T
        size = size + jnp.sum(l[c])

    @pl.when(jnp.logical_not(size < MLA_MAX_FAST_SUM))
    def _():
        acc_ref[...] = jnp.zeros(acc_ref.shape, F32)

        def body(j, carry):
            m_prev, l_prev = carry
            s = scores(j)
            m_new = jnp.maximum(m_prev, jnp.max(s, axis=0, keepdims=True))
            alpha = jnp.exp2(m_prev - m_new)
            p = jnp.exp2(s - m_new)
            acc_ref[...] = alpha * acc_ref[...] + pv(j, p)
            return m_new, alpha * l_prev + jnp.sum(p, axis=0, keepdims=True)

        init = (jnp.full((1, tq), -jnp.inf, F32), jnp.zeros((1, tq), F32))
        _, l_exact = lax.fori_loop(0, nk, body, init)
        o_ref[...] = (acc_ref[...] / l_exact).T


def _mla_attention(q, k, vt, tq=1024):
    s = q.shape[0]
    nk, _, tk = vt.shape
    tq = min(tq, s)
    return pl.pallas_call(
        _mla_kernel,
        grid=(MLA_HEADS, s // tq),
        in_specs=[
            pl.BlockSpec((tq, MLA_QK_PAD), lambda h, i: (i, h)),
            pl.BlockSpec((s, MLA_QK_PAD), lambda h, i: (0, h)),
            pl.BlockSpec((nk, MLA_V, tk), lambda h, i: (0, h, 0)),
        ],
        out_specs=pl.BlockSpec((tq, MLA_V), lambda h, i: (i, h)),
        out_shape=jax.ShapeDtypeStruct((s, MLA_WIDTH), F32),
        scratch_shapes=[pltpu.VMEM((MLA_V, tq), F32), pltpu.VMEM((2, tk, min(MLA_PANEL, tq)), BF16)],
        compiler_params=_cparams(("parallel", "parallel")),
        name="mla_attention",
    )(q, k, vt)


def _xattn_kernel(q_ref, k_ref, v_ref, o_ref, *, d_model, scale):
    dh = d_model // X_HEADS
    for h in range(X_HEADS):
        q = q_ref[:, h * dh:(h + 1) * dh]
        k = k_ref[:, h * dh:(h + 1) * dh]
        v = v_ref[:, h * dh:(h + 1) * dh]
        s = _nt_dot(q, k) * scale
        m = jnp.max(s, axis=-1, keepdims=True)
        p = jnp.exp(s - m)
        l = jnp.sum(p, axis=-1, keepdims=True)
        o = jnp.dot(p.astype(BF16), v, preferred_element_type=F32) / l
        o_ref[:, h * dh:(h + 1) * dh] = o.astype(o_ref.dtype)


def _xattn(q, k, v, tq=512):
    s, d_model = q.shape
    tq = min(tq, s)
    return pl.pallas_call(
        functools.partial(_xattn_kernel, d_model=d_model, scale=(d_model // X_HEADS) ** -0.5),
        grid=(s // tq,),
        in_specs=[
            pl.BlockSpec((tq, d_model), lambda i: (i, 0)),
            pl.BlockSpec(k.shape, lambda i: (0, 0)),
            pl.BlockSpec(v.shape, lambda i: (0, 0)),
        ],
        out_specs=pl.BlockSpec((tq, d_model), lambda i: (i, 0)),
        out_shape=jax.ShapeDtypeStruct((s, d_model), BF16),
        compiler_params=_cparams(("parallel",)),
        name="mem_attention",
    )(q, k, v)


def _ffn_kernel(x_ref, g_ref, wg_ref, wu_ref, wd_ref, *rest, out_norm):
    og_ref = rest[0] if out_norm else None
    o_ref, xn_ref = rest[-2:]
    f = pl.program_id(1)

    @pl.when(f == 0)
    def _():
        x = x_ref[...]
        xn_ref[...] = _rms(x, g_ref[...]).astype(BF16)
        o_ref[...] = x

    xn = xn_ref[...]
    gate = jnp.dot(xn, wg_ref[...], preferred_element_type=F32)
    up = jnp.dot(xn, wu_ref[...], preferred_element_type=F32)
    act = (gate * jax.nn.sigmoid(gate) * up).astype(BF16)
    o_ref[...] += jnp.dot(act, wd_ref[...], preferred_element_type=F32)

    if out_norm:
        @pl.when(f == pl.num_programs(1) - 1)
        def _():
            o_ref[...] = _rms(o_ref[...], og_ref[...])


def _ffn(x, gain, wg, wu, wd, layer, out_gain=None, tm=1024, tf=512):
    s, d = x.shape
    f = wg.shape[2]
    tm = min(tm, s)
    assert s % tm == 0 and f % tf == 0
    in_specs = [
        pl.BlockSpec((tm, d), lambda i, j: (i, 0)),
        pl.BlockSpec((1, d), lambda i, j: (0, 0)),
        pl.BlockSpec((None, d, tf), lambda i, j: (layer, 0, j)),
        pl.BlockSpec((None, d, tf), lambda i, j: (layer, 0, j)),
        pl.BlockSpec((None, tf, d), lambda i, j: (layer, j, 0)),
    ]
    args = [x, gain.reshape(1, d), wg, wu, wd]
    if out_gain is not None:
        in_specs.append(pl.BlockSpec((1, d), lambda i, j: (0, 0)))
        args.append(out_gain.reshape(1, d))
    return pl.pallas_call(
        functools.partial(_ffn_kernel, out_norm=out_gain is not None),
        grid=(s // tm, f // tf),
        in_specs=in_specs,
        out_specs=pl.BlockSpec((tm, d), lambda i, j: (i, 0)),
        out_shape=jax.ShapeDtypeStruct((s, d), F32),
        scratch_shapes=[pltpu.VMEM((tm, d), BF16)],
        compiler_params=_cparams(("parallel", "arbitrary")),
        name="ffn",
    )(*args)


def _latent_w_in(w_in):
    o3 = 3 * NA_WIDTH
    o5 = o3 + MLA_Q_RANK + MLA_KV_RANK
    half = MLA_ROPE // 2
    z = jnp.zeros(w_in.shape[:2] + (LANES - MLA_ROPE,), w_in.dtype)
    w_lat = jnp.concatenate(
        [w_in[..., o3:], z, w_in[..., o5 + half:], w_in[..., o5:o5 + half], z], axis=-1)
    return w_lat.astype(BF16)


def _split_w_uq(w_uq):
    lead = w_uq.shape[:2]
    half = MLA_ROPE // 2
    w = w_uq.reshape(lead + (MLA_HEADS, MLA_QK))
    wp = jnp.pad(w, ((0, 0), (0, 0), (0, 0), (0, MLA_QK_PAD - MLA_QK)))
    ws = jnp.concatenate(
        [w[..., MLA_NOPE + half:], w[..., MLA_NOPE:MLA_NOPE + half],
         jnp.zeros(lead + (MLA_HEADS, LANES - MLA_ROPE), w.dtype)], axis=-1)
    return (wp.reshape(lead + (MLA_HEADS * MLA_QK_PAD,)).astype(BF16),
            ws.reshape(lead + (MLA_HEADS * LANES,)).astype(BF16))


def _split_w_ukv(w_ukv):
    lead = w_ukv.shape[:2]
    w = w_ukv.reshape(lead + (MLA_HEADS, MLA_NOPE + MLA_V))
    wk = w[..., :MLA_NOPE].reshape(lead + (MLA_HEADS * MLA_NOPE,))
    wv = w[..., MLA_NOPE:].reshape(lead + (MLA_HEADS * MLA_V,))
    return wk.astype(BF16), jnp.swapaxes(wv, 1, 2).astype(BF16)


def kernel(x, mem, ln_mix, w_in, q_norm, kv_norm, w_uq, w_ukv, na_rpb, na_out_norm, mla_out_norm,
           w_out, ln_mem, mem_norm, w_xq, w_xk, w_xv, w_xo, ln_ffn, w_gate, w_up, w_down,
           final_norm):
    b, s, d = x.shape
    assert b == 1
    depth = w_in.shape[0]
    x = x.reshape(s, d)
    mem2 = mem.reshape(mem.shape[1], d)
    tab = _rope_tables(s, MLA_SCORE_SCALE)
    w_in_b, w_lat = w_in.astype(BF16), _latent_w_in(w_in)
    wq, wqs = _split_w_uq(w_uq)
    wk, wvt = _split_w_ukv(w_ukv)
    bias_rows = _na_bias_rows(na_rpb)
    w_out_b, w_xq_b, w_xk_b, w_xv_b, w_xo_b = (
        w.astype(BF16) for w in (w_out, w_xq, w_xk, w_xv, w_xo))
    w_gate_b, w_up_b, w_down_b = (w.astype(BF16) for w in (w_gate, w_up, w_down))
    for l in range(depth):
        qkv, c = _nmm([x], [ln_mix[l]], [w_in_b, w_lat], l, out_dtypes=(BF16, F32),
                      splits=(3, 1), ncols=[3 * NA_WIDTH, w_lat.shape[-1]],
                      first_scale=NA_SCORE_SCALE)
        y_na = _na_attention(qkv, bias_rows[l])
        ql, kl, vl = _latent_up(c, q_norm[l], kv_norm[l], wq[l], wqs[l], wk[l], wvt[l], tab,
                                tm=min(MLA_KV_CHUNK, s))
        y_mla = _mla_attention(ql, kl, vl)
        x = _nmm([y_na, y_mla], [na_out_norm[l], mla_out_norm[l]], [w_out_b], l, res=x)
        qx = _nmm([x], [ln_mem[l]], [w_xq_b], l, out_dtypes=(BF16,))
        kx, vx = _nmm([mem2], [mem_norm[l]], [w_xk_b, w_xv_b], l, out_dtypes=(BF16, BF16),
                      splits=(1, 1))
        ox = _xattn(qx, kx, vx)
        x = _nmm([ox], [None], [w_xo_b], l, res=x)
        x = _ffn(x, ln_ffn[l], w_gate_b, w_up_b, w_down_b, l,
                 out_gain=final_norm if l == depth - 1 else None)
    return x.reshape(b, s, d)
```

```python
import functools

import jax
import jax.numpy as jnp
import numpy as np
from jax import lax
from jax.experimental import pallas as pl
from jax.experimental.pallas import tpu as pltpu

F32 = jnp.float32
BF16 = jnp.bfloat16

EPS = 1e-6
GRID_W = 64
NA_HEADS = 8
NA_HEAD_DIM = 128
NA_WIDTH = NA_HEADS * NA_HEAD_DIM
NA_ROWS = 8
NA_COLS = 16
MLA_HEADS = 8
MLA_Q_RANK = 512
MLA_KV_RANK = 256
MLA_NOPE = 128
MLA_ROPE = 64
MLA_V = 128
MLA_QK = MLA_NOPE + MLA_ROPE
MLA_QK_PAD = 256
MLA_WIDTH = MLA_HEADS * MLA_V
MLA_SCORE_SCALE = MLA_QK ** -0.5 * float(np.log2(np.e))
MLA_KV_CHUNK = 1024
MLA_PANEL = 512
MLA_ROW_BLOCK = 16
MLA_MAX_FAST_SUM = 2.0 ** 100
ROPE_THETA = 10000.0
X_HEADS = 4
LANES = 128
VMEM_LIMIT = 56 * 1024 * 1024

NA_GROUP_ROWS = 4
NA_WIN_ROWS = NA_GROUP_ROWS + NA_ROWS
NEG_INF = -1e30
NA_LOG2E = float(np.log2(np.e))
NA_SCORE_SCALE = NA_HEAD_DIM ** -0.5 * NA_LOG2E


def _cparams(sem):
    return pltpu.CompilerParams(dimension_semantics=sem, vmem_limit_bytes=VMEM_LIMIT)


def _nt_dot(a, b):
    return lax.dot_general(a, b, (((1,), (1,)), ((), ())), preferred_element_type=F32)


def _rms(x, g):
    return x * lax.rsqrt(jnp.mean(x * x, axis=-1, keepdims=True) + EPS) * g


def _nmm_kernel(*refs, widths, normed, splits, has_res, sub, first_scale):
    n = len(widths)
    nw = len(splits)
    x_refs = refs[:n]
    g_refs = refs[n:n + sum(normed)]
    rest = refs[n + sum(normed):]
    w_refs = rest[:nw]
    r_ref = rest[nw] if has_res else None
    o_refs = rest[nw + has_res:]
    tm = x_refs[0].shape[0]
    for r0 in range(0, tm, sub):
        rows = slice(r0, r0 + sub)
        xs = []
        gi = 0
        for p in range(n):
            x = x_refs[p][rows, :]
            if normed[p]:
                x = _rms(x.astype(F32), g_refs[gi][...])
                gi += 1
            xs.append(x.astype(BF16))
        for w_ref, o_ref, split in zip(w_refs, o_refs, splits):
            off = 0
            acc = None
            ncols = o_ref.shape[-1] * split
            for p in range(n):
                d = jnp.dot(xs[p], w_ref[off:off + widths[p], :ncols], preferred_element_type=F32)
                acc = d if acc is None else acc + d
                off += widths[p]
            if has_res:
                acc = acc + r_ref[rows, :]
            if split == 1:
                o_ref[rows, :] = acc.astype(o_ref.dtype)
            else:
                wd = acc.shape[1] // split
                for c in range(split):
                    piece = acc[:, c * wd:(c + 1) * wd]
                    if c == 0 and first_scale is not None:
                        piece = piece * first_scale
                    o_ref[c, rows, :] = piece.astype(o_ref.dtype)


def _nmm(parts, gains, ws, layer, res=None, out_dtypes=(F32,), splits=(1,), ncols=None,
         first_scale=None, tm=512, sub=256):
    m = parts[0].shape[0]
    widths = tuple(p.shape[1] for p in parts)
    k = sum(widths)
    ncols = [w.shape[2] for w in ws] if ncols is None else ncols
    assert all(w.shape[1] == k for w in ws) and (res is None or len(ws) == 1)
    tm = min(tm, m)
    sub = min(sub, tm)
    assert m % tm == 0 and tm % sub == 0
    normed = tuple(g is not None for g in gains)
    row = lambda i: (i, 0)
    const = lambda i: (0, 0)
    in_specs = [pl.BlockSpec((tm, wd), row) for wd in widths]
    args = list(parts)
    for g, wd in zip(gains, widths):
        if g is not None:
            in_specs.append(pl.BlockSpec((1, wd), const))
            args.append(g.reshape(1, wd).astype(F32))
    for w in ws:
        in_specs.append(pl.BlockSpec((None,) + w.shape[1:], lambda i: (layer, 0, 0),
                                     pipeline_mode=pl.Buffered(1)))
        args.append(w)
    if res is not None:
        in_specs.append(pl.BlockSpec((tm, ncols[0]), row))
        args.append(res)
    out_specs, out_shapes = [], []
    for n, dt, split in zip(ncols, out_dtypes, splits):
        if split > 1:
            out_specs.append(pl.BlockSpec((split, tm, n // split), lambda i: (0, i, 0)))
            out_shapes.append(jax.ShapeDtypeStruct((split, m, n // split), dt))
        else:
            out_specs.append(pl.BlockSpec((tm, n), row))
            out_shapes.append(jax.ShapeDtypeStruct((m, n), dt))
    outs = pl.pallas_call(
        functools.partial(_nmm_kernel, widths=widths, normed=normed, splits=tuple(splits),
                          has_res=res is not None, sub=sub, first_scale=first_scale),
        grid=(m // tm,),
        in_specs=in_specs,
        out_specs=out_specs,
        out_shape=out_shapes,
        compiler_params=_cparams(("parallel",)),
        name="nmm",
    )(*args)
    return outs if len(outs) > 1 else outs[0]


def _na_kernel(q_ref, k_ref, v_ref, t_ref, o_ref, b_ref, *, patterns):
    g = pl.program_id(0)
    groups = pl.num_programs(0)

    def build(pattern):
        b_ref[...] = jnp.full(b_ref.shape, NEG_INF, F32)
        for i, (dr0, koff) in enumerate(pattern):
            for e in range(NA_ROWS):
                a = koff + e
                half = slice((a % 2) * GRID_W, (a % 2 + 1) * GRID_W)
                b_ref[:, i * GRID_W:(i + 1) * GRID_W, a * GRID_W:(a + 1) * GRID_W] = (
                    t_ref[:, dr0 + e, :, half])

    first, interior, last = patterns
    pl.when(g == 0)(lambda: build(first))
    pl.when(g == 1)(lambda: build(interior))
    pl.when(g == groups - 1)(lambda: build(last))

    for h in range(NA_HEADS):
        hs = slice(h * NA_HEAD_DIM, (h + 1) * NA_HEAD_DIM)
        s = _nt_dot(q_ref[:, hs], k_ref[0, :, hs]) + b_ref[h]
        m = jnp.max(s, axis=-1, keepdims=True)
        p = jnp.exp2(s - m)
        l = jnp.sum(p, axis=-1, keepdims=True)
        o = jnp.dot(p.astype(BF16), v_ref[0, :, hs], preferred_element_type=F32)
        o_ref[:, hs] = o / l


def _na_group_layout(rows):
    win0, patterns = [], []
    for g in range(rows // NA_GROUP_ROWS):
        w0 = int(np.clip(g * NA_GROUP_ROWS - NA_ROWS // 2, 0, rows - NA_WIN_ROWS))
        pat = []
        for i in range(NA_GROUP_ROWS):
            r = g * NA_GROUP_ROWS + i
            rs = int(np.clip(r - NA_ROWS // 2, 0, rows - NA_ROWS))
            pat.append((rs - r + NA_ROWS - 1, rs - w0))
        win0.append(w0)
        patterns.append(tuple(pat))
    return win0, patterns


def _na_bias_rows(rpb):
    qc = np.arange(GRID_W)
    col_start = np.clip(qc - NA_COLS // 2, 0, GRID_W - NA_COLS)
    kc = np.arange(GRID_W)
    inside = (kc[None, :] >= col_start[:, None]) & (kc[None, :] < col_start[:, None] + NA_COLS)
    period = 2 * GRID_W
    lead = rpb.shape[:-1]
    u = jnp.concatenate(
        [rpb[..., NA_COLS - 1:], jnp.zeros(lead + (period - 2 * NA_COLS + 1,), rpb.dtype),
         rpb[..., :NA_COLS - 1]], axis=-1)
    flat = jnp.tile(u, GRID_W)[..., :GRID_W * (period - 1)]
    toeplitz = flat.reshape(lead + (GRID_W, period - 1))[..., :GRID_W]
    full = jnp.where(inside, toeplitz * NA_LOG2E, NEG_INF).astype(F32)
    return jnp.concatenate([full, full], axis=-1)


def _na_attention(qkv, bias_rows):
    s = qkv.shape[1]
    rows = s // GRID_W
    assert rows % NA_GROUP_ROWS == 0 and rows >= NA_WIN_ROWS + NA_GROUP_ROWS
    groups = rows // NA_GROUP_ROWS
    tq = NA_GROUP_ROWS * GRID_W
    win = NA_WIN_ROWS * GRID_W
    _, patterns = _na_group_layout(rows)
    assert groups >= 3 and all(p == patterns[1] for p in patterns[1:-1])
    patterns = (patterns[0], patterns[1], patterns[-1])

    def kv_map(which):
        def index(g):
            r0 = jnp.clip(NA_GROUP_ROWS * g - NA_ROWS // 2, 0, rows - NA_WIN_ROWS)
            return (which, r0 * GRID_W, 0)
        return index

    kv_block = (pl.Element(1), pl.Element(win), pl.Element(NA_WIDTH))
    return pl.pallas_call(
        functools.partial(_na_kernel, patterns=patterns),
        grid=(groups,),
        in_specs=[
            pl.BlockSpec((None, tq, NA_WIDTH), lambda g: (0, g, 0)),
            pl.BlockSpec(kv_block, kv_map(1)),
            pl.BlockSpec(kv_block, kv_map(2)),
            pl.BlockSpec(bias_rows.shape, lambda g: (0, 0, 0, 0)),
        ],
        out_specs=pl.BlockSpec((tq, NA_WIDTH), lambda g: (g, 0)),
        out_shape=jax.ShapeDtypeStruct((s, NA_WIDTH), F32),
        scratch_shapes=[pltpu.VMEM((NA_HEADS, tq, win), F32)],
        compiler_params=_cparams(("arbitrary",)),
        name="na_attention",
    )(qkv, qkv, qkv, bias_rows)


def _latup_kernel(c_ref, qg_ref, kvg_ref, wq_ref, wqs_ref, wk_ref, wv_ref, tab_ref,
                  q_ref, k_ref, v_ref, *, scale):
    c = c_ref[...]
    qn = _rms(c[:, :MLA_Q_RANK], qg_ref[...]).astype(BF16)
    kvn = _rms(c[:, MLA_Q_RANK:MLA_Q_RANK + MLA_KV_RANK], kvg_ref[...]).astype(BF16)
    a = jnp.dot(qn, wq_ref[...], preferred_element_type=F32)
    b = jnp.dot(qn, wqs_ref[...], preferred_element_type=F32)
    kn = jnp.dot(kvn, wk_ref[...], preferred_element_type=F32)
    v_ref[...] = _nt_dot(wv_ref[...], kvn).astype(BF16)
    tab = tab_ref[...]
    cq, sq = tab[:, :LANES], tab[:, LANES:2 * LANES]
    ck, sk = tab[:, 2 * LANES:3 * LANES], tab[:, 3 * LANES:]
    r0 = MLA_Q_RANK + MLA_KV_RANK
    kr = (c[:, r0:r0 + LANES] * ck + c[:, r0 + LANES:r0 + 2 * LANES] * sk).astype(BF16)
    for h in range(MLA_HEADS):
        o = h * MLA_QK_PAD
        q_ref[:, o:o + LANES] = (a[:, o:o + LANES] * scale).astype(BF16)
        q_ref[:, o + LANES:o + 2 * LANES] = (
            a[:, o + LANES:o + 2 * LANES] * cq + b[:, h * LANES:(h + 1) * LANES] * sq).astype(BF16)
        k_ref[:, o:o + LANES] = kn[:, h * LANES:(h + 1) * LANES].astype(BF16)
        k_ref[:, o + LANES:o + 2 * LANES] = kr


def _latent_up(c, q_gain, kv_gain, wq, wqs, wk, wvt, tab, tm):
    s = c.shape[0]
    assert s % tm == 0
    const = lambda i: (0, 0)
    row = lambda i: (i, 0)
    return pl.pallas_call(
        functools.partial(_latup_kernel, scale=MLA_SCORE_SCALE),
        grid=(s // tm,),
        in_specs=[
            pl.BlockSpec((tm, c.shape[1]), row),
            pl.BlockSpec((1, MLA_Q_RANK), const),
            pl.BlockSpec((1, MLA_KV_RANK), const),
            pl.BlockSpec(wq.shape, const),
            pl.BlockSpec(wqs.shape, const),
            pl.BlockSpec(wk.shape, const),
            pl.BlockSpec(wvt.shape, const),
            pl.BlockSpec((tm, 4 * LANES), row),
        ],
        out_specs=[
            pl.BlockSpec((tm, MLA_HEADS * MLA_QK_PAD), row),
            pl.BlockSpec((tm, MLA_HEADS * MLA_QK_PAD), row),
            pl.BlockSpec((None, MLA_WIDTH, tm), lambda i: (i, 0, 0)),
        ],
        out_shape=[
            jax.ShapeDtypeStruct((s, MLA_HEADS * MLA_QK_PAD), BF16),
            jax.ShapeDtypeStruct((s, MLA_HEADS * MLA_QK_PAD), BF16),
            jax.ShapeDtypeStruct((s // tm, MLA_WIDTH, tm), BF16),
        ],
        compiler_params=_cparams(("parallel",)),
        name="latent_up",
    )(c, q_gain.reshape(1, -1), kv_gain.reshape(1, -1), wq, wqs, wk, wvt, tab)


def _rope_tables(s, scale):
    half = MLA_ROPE // 2
    inv = 1.0 / (ROPE_THETA ** (jnp.arange(0, MLA_ROPE, 2, dtype=F32) / MLA_ROPE))
    ang = jnp.arange(s, dtype=F32)[:, None] * inv[None, :]
    cos, sin = jnp.cos(ang), jnp.sin(ang)
    z = jnp.zeros((s, LANES - 2 * half), F32)
    ct = jnp.concatenate([cos, cos, z], axis=1)
    st = jnp.concatenate([-sin, sin, z], axis=1)
    return jnp.concatenate([ct * scale, st * scale, ct, st], axis=1)


def _mla_kernel(q_ref, k_ref, vt_ref, o_ref, acc_ref, p_ref):
    nk, _, tk = vt_ref.shape
    tq = q_ref.shape[0]
    q = q_ref[...]

    def scores(j):
        return _nt_dot(k_ref[pl.ds(pl.multiple_of(j * tk, tk), tk), :], q)

    def pv(j, p):
        return jnp.dot(vt_ref[j], p.astype(BF16), preferred_element_type=F32)

    def fold8(v, op):
        parts = [v[i:i + 8] for i in range(0, v.shape[0], 8)]
        while len(parts) > 1:
            parts = [op(parts[i], parts[i + 1]) for i in range(0, len(parts), 2)]
        return parts[0]

    def stream_softmax(s, m_row, p_dst):
        m_blk = jnp.broadcast_to(m_row, (MLA_ROW_BLOCK, s.shape[1]))
        csum = None
        for r in range(0, tk, MLA_ROW_BLOCK):
            e = jnp.exp2(s[r:r + MLA_ROW_BLOCK] - m_blk)
            es = fold8(e, jnp.add)
            csum = es if csum is None else csum + es
            p_dst[r:r + MLA_ROW_BLOCK, :] = e.astype(BF16)
        return jnp.sum(csum, axis=0, keepdims=True)

    panels = [slice(c, c + MLA_PANEL) for c in range(0, tq, MLA_PANEL)]
    items = [(j, c) for j in range(nk) for c in range(len(panels))]

    def panel_scores(item):
        j, c = item
        return _nt_dot(k_ref[j * tk:(j + 1) * tk, :], q_ref[panels[c], :])

    m, l = {}, {}
    for w, (j, c) in enumerate(items):
        cs = panels[c]
        s = panel_scores((j, c))
        if j == 0:
            m[c] = jnp.max(fold8(s[:MLA_ROW_BLOCK], jnp.maximum), axis=0, keepdims=True)
        p_dst = p_ref.at[w % 2, :, :]
        chunk_sum = stream_softmax(s, m[c], p_dst)
        part = jnp.dot(vt_ref[j], p_dst[...], preferred_element_type=F32)
        if j == 0:
            l[c] = chunk_sum
            acc_ref[:, cs] = part
        else:
            l[c] = l[c] + chunk_sum
            acc_ref[:, cs] = acc_ref[:, cs] + part
    size = jnp.sum(jnp.abs(acc_ref[...]))
    for c, cs in enumerate(panels):
        o_ref[cs, :] = (acc_ref[:, cs] / l[c]).T
        size = size + jnp.sum(l[c])

    @pl.when(jnp.logical_not(size < MLA_MAX_FAST_SUM))
    def _():
        acc_ref[...] = jnp.zeros(acc_ref.shape, F32)

        def body(j, carry):
            m_prev, l_prev = carry
            s = scores(j)
            m_new = jnp.maximum(m_prev, jnp.max(s, axis=0, keepdims=True))
            alpha = jnp.exp2(m_prev - m_new)
            p = jnp.exp2(s - m_new)
            acc_ref[...] = alpha * acc_ref[...] + pv(j, p)
            return m_new, alpha * l_prev + jnp.sum(p, axis=0, keepdims=True)

        init = (jnp.full((1, tq), -jnp.inf, F32), jnp.zeros((1, tq), F32))
        _, l_exact = lax.fori_loop(0, nk, body, init)
        o_ref[...] = (acc_ref[...] / l_exact).T


def _mla_attention(q, k, vt, tq=2048):
    s = q.shape[0]
    nk, _, tk = vt.shape
    tq = min(tq, s)
    return pl.pallas_call(
        _mla_kernel,
        grid=(MLA_HEADS, s // tq),
        in_specs=[
            pl.BlockSpec((tq, MLA_QK_PAD), lambda h, i: (i, h)),
            pl.BlockSpec((s, MLA_QK_PAD), lambda h, i: (0, h)),
            pl.BlockSpec((nk, MLA_V, tk), lambda h, i: (0, h, 0)),
        ],
        out_specs=pl.BlockSpec((tq, MLA_V), lambda h, i: (i, h)),
        out_shape=jax.ShapeDtypeStruct((s, MLA_WIDTH), F32),
        scratch_shapes=[pltpu.VMEM((MLA_V, tq), F32), pltpu.VMEM((2, tk, min(MLA_PANEL, tq)), BF16)],
        compiler_params=_cparams(("parallel", "parallel")),
        name="mla_attention",
    )(q, k, vt)


def _xattn_kernel(q_ref, k_ref, v_ref, o_ref, *, d_model, scale):
    dh = d_model // X_HEADS
    for h in range(X_HEADS):
        q = q_ref[:, h * dh:(h + 1) * dh]
        k = k_ref[:, h * dh:(h + 1) * dh]
        v = v_ref[:, h * dh:(h + 1) * dh]
        s = _nt_dot(q, k) * scale
        m = jnp.max(s, axis=-1, keepdims=True)
        p = jnp.exp(s - m)
        l = jnp.sum(p, axis=-1, keepdims=True)
        o = jnp.dot(p.astype(BF16), v, preferred_element_type=F32) / l
        o_ref[:, h * dh:(h + 1) * dh] = o.astype(o_ref.dtype)


def _xattn(q, k, v, tq=512):
    s, d_model = q.shape
    tq = min(tq, s)
    return pl.pallas_call(
        functools.partial(_xattn_kernel, d_model=d_model, scale=(d_model // X_HEADS) ** -0.5),
        grid=(s // tq,),
        in_specs=[
            pl.BlockSpec((tq, d_model), lambda i: (i, 0)),
            pl.BlockSpec(k.shape, lambda i: (0, 0)),
            pl.BlockSpec(v.shape, lambda i: (0, 0)),
        ],
        out_specs=pl.BlockSpec((tq, d_model), lambda i: (i, 0)),
        out_shape=jax.ShapeDtypeStruct((s, d_model), BF16),
        compiler_params=_cparams(("parallel",)),
        name="mem_attention",
    )(q, k, v)


def _ffn_kernel(x_ref, g_ref, wg_ref, wu_ref, wd_ref, *rest, out_norm):
    og_ref = rest[0] if out_norm else None
    o_ref, xn_ref = rest[-2:]
    f = pl.program_id(1)

    @pl.when(f == 0)
    def _():
        x = x_ref[...]
        xn_ref[...] = _rms(x, g_ref[...]).astype(BF16)
        o_ref[...] = x

    xn = xn_ref[...]
    gate = jnp.dot(xn, wg_ref[...], preferred_element_type=F32)
    up = jnp.dot(xn, wu_ref[...], preferred_element_type=F32)
    act = (gate * jax.nn.sigmoid(gate) * up).astype(BF16)
    o_ref[...] += jnp.dot(act, wd_ref[...], preferred_element_type=F32)

    if out_norm:
        @pl.when(f == pl.num_programs(1) - 1)
        def _():
            o_ref[...] = _rms(o_ref[...], og_ref[...])


def _ffn(x, gain, wg, wu, wd, layer, out_gain=None, tm=1024, tf=512):
    s, d = x.shape
    f = wg.shape[2]
    tm = min(tm, s)
    assert s % tm == 0 and f % tf == 0
    in_specs = [
        pl.BlockSpec((tm, d), lambda i, j: (i, 0)),
        pl.BlockSpec((1, d), lambda i, j: (0, 0)),
        pl.BlockSpec((None, d, tf), lambda i, j: (layer, 0, j)),
        pl.BlockSpec((None, d, tf), lambda i, j: (layer, 0, j)),
        pl.BlockSpec((None, tf, d), lambda i, j: (layer, j, 0)),
    ]
    args = [x, gain.reshape(1, d), wg, wu, wd]
    if out_gain is not None:
        in_specs.append(pl.BlockSpec((1, d), lambda i, j: (0, 0)))
        args.append(out_gain.reshape(1, d))
    return pl.pallas_call(
        functools.partial(_ffn_kernel, out_norm=out_gain is not None),
        grid=(s // tm, f // tf),
        in_specs=in_specs,
        out_specs=pl.BlockSpec((tm, d), lambda i, j: (i, 0)),
        out_shape=jax.ShapeDtypeStruct((s, d), F32),
        scratch_shapes=[pltpu.VMEM((tm, d), BF16)],
        compiler_params=_cparams(("parallel", "arbitrary")),
        name="ffn",
    )(*args)


def _latent_w_in(w_in):
    o3 = 3 * NA_WIDTH
    o5 = o3 + MLA_Q_RANK + MLA_KV_RANK
    half = MLA_ROPE // 2
    z = jnp.zeros(w_in.shape[:2] + (LANES - MLA_ROPE,), w_in.dtype)
    w_lat = jnp.concatenate(
        [w_in[..., o3:], z, w_in[..., o5 + half:], w_in[..., o5:o5 + half], z], axis=-1)
    return w_lat.astype(BF16)


def _split_w_uq(w_uq):
    lead = w_uq.shape[:2]
    half = MLA_ROPE // 2
    w = w_uq.reshape(lead + (MLA_HEADS, MLA_QK))
    wp = jnp.pad(w, ((0, 0), (0, 0), (0, 0), (0, MLA_QK_PAD - MLA_QK)))
    ws = jnp.concatenate(
        [w[..., MLA_NOPE + half:], w[..., MLA_NOPE:MLA_NOPE + half],
         jnp.zeros(lead + (MLA_HEADS, LANES - MLA_ROPE), w.dtype)], axis=-1)
    return (wp.reshape(lead + (MLA_HEADS * MLA_QK_PAD,)).astype(BF16),
            ws.reshape(lead + (MLA_HEADS * LANES,)).astype(BF16))


def _split_w_ukv(w_ukv):
    lead = w_ukv.shape[:2]
    w = w_ukv.reshape(lead + (MLA_HEADS, MLA_NOPE + MLA_V))
    wk = w[..., :MLA_NOPE].reshape(lead + (MLA_HEADS * MLA_NOPE,))
    wv = w[..., MLA_NOPE:].reshape(lead + (MLA_HEADS * MLA_V,))
    return wk.astype(BF16), jnp.swapaxes(wv, 1, 2).astype(BF16)


def kernel(x, mem, ln_mix, w_in, q_norm, kv_norm, w_uq, w_ukv, na_rpb, na_out_norm, mla_out_norm,
           w_out, ln_mem, mem_norm, w_xq, w_xk, w_xv, w_xo, ln_ffn, w_gate, w_up, w_down,
           final_norm):
    b, s, d = x.shape
    assert b == 1
    depth = w_in.shape[0]
    x = x.reshape(s, d)
    mem2 = mem.reshape(mem.shape[1], d)
    tab = _rope_tables(s, MLA_SCORE_SCALE)
    w_in_b, w_lat = w_in.astype(BF16), _latent_w_in(w_in)
    wq, wqs = _split_w_uq(w_uq)
    wk, wvt = _split_w_ukv(w_ukv)
    bias_rows = _na_bias_rows(na_rpb)
    w_out_b, w_xq_b, w_xk_b, w_xv_b, w_xo_b = (
        w.astype(BF16) for w in (w_out, w_xq, w_xk, w_xv, w_xo))
    w_gate_b, w_up_b, w_down_b = (w.astype(BF16) for w in (w_gate, w_up, w_down))
    for l in range(depth):
        qkv, c = _nmm([x], [ln_mix[l]], [w_in_b, w_lat], l, out_dtypes=(BF16, F32),
                      splits=(3, 1), ncols=[3 * NA_WIDTH, w_lat.shape[-1]],
                      first_scale=NA_SCORE_SCALE)
        y_na = _na_attention(qkv, bias_rows[l])
        ql, kl, vl = _latent_up(c, q_norm[l], kv_norm[l], wq[l], wqs[l], wk[l], wvt[l], tab,
                                tm=min(MLA_KV_CHUNK, s))
        y_mla = _mla_attention(ql, kl, vl)
        x = _nmm([y_na, y_mla], [na_out_norm[l], mla_out_norm[l]], [w_out_b], l, res=x)
        qx = _nmm([x], [ln_mem[l]], [w_xq_b], l, out_dtypes=(BF16,))
        kx, vx = _nmm([mem2], [mem_norm[l]], [w_xk_b, w_xv_b], l, out_dtypes=(BF16, BF16),
                      splits=(1, 1))
        ox = _xattn(qx, kx, vx)
        x = _nmm([ox], [None], [w_xo_b], l, res=x)
        x = _ffn(x, ln_ffn[l], w_gate_b, w_up_b, w_down_b, l,
                 out_gain=final_norm if l == depth - 1 else None)
    return x.reshape(b, s, d)
```

```python
import functools

import jax
import jax.numpy as jnp
import numpy as np
from jax import lax
from jax.experimental import pallas as pl
from jax.experimental.pallas import tpu as pltpu

F32 = jnp.float32
BF16 = jnp.bfloat16

EPS = 1e-6
GRID_W = 64
NA_HEADS = 8
NA_HEAD_DIM = 128
NA_WIDTH = NA_HEADS * NA_HEAD_DIM
NA_ROWS = 8
NA_COLS = 16
MLA_HEADS = 8
MLA_Q_RANK = 512
MLA_KV_RANK = 256
MLA_NOPE = 128
MLA_ROPE = 64
MLA_V = 128
MLA_QK = MLA_NOPE + MLA_ROPE
MLA_QK_PAD = 256
MLA_WIDTH = MLA_HEADS * MLA_V
MLA_SCORE_SCALE = MLA_QK ** -0.5 * float(np.log2(np.e))
MLA_KV_CHUNK = 1024
MLA_PANEL = 512
MLA_ROW_BLOCK = 16
MLA_MAX_FAST_SUM = 2.0 ** 100
ROPE_THETA = 10000.0
X_HEADS = 4
LANES = 128
VMEM_LIMIT = 56 * 1024 * 1024

NA_GROUP_ROWS = 4
NA_WIN_ROWS = NA_GROUP_ROWS + NA_ROWS
NEG_INF = -1e30
NA_LOG2E = float(np.log2(np.e))
NA_SCORE_SCALE = NA_HEAD_DIM ** -0.5 * NA_LOG2E


def _cparams(sem):
    return pltpu.CompilerParams(dimension_semantics=sem, vmem_limit_bytes=VMEM_LIMIT)


def _nt_dot(a, b):
    return lax.dot_general(a, b, (((1,), (1,)), ((), ())), preferred_element_type=F32)


def _rms(x, g):
    return x * lax.rsqrt(jnp.mean(x * x, axis=-1, keepdims=True) + EPS) * g


def _nmm_kernel(*refs, widths, normed, splits, has_res, sub, first_scale):
    n = len(widths)
    nw = len(splits)
    x_refs = refs[:n]
    g_refs = refs[n:n + sum(normed)]
    rest = refs[n + sum(normed):]
    w_refs = rest[:nw]
    r_ref = rest[nw] if has_res else None
    o_refs = rest[nw + has_res:]
    tm = x_refs[0].shape[0]
    for r0 in range(0, tm, sub):
        rows = slice(r0, r0 + sub)
        xs = []
        gi = 0
        for p in range(n):
            x = x_refs[p][rows, :]
            if normed[p]:
                x = _rms(x.astype(F32), g_refs[gi][...])
                gi += 1
            xs.append(x.astype(BF16))
        for w_ref, o_ref, split in zip(w_refs, o_refs, splits):
            off = 0
            acc = None
            ncols = o_ref.shape[-1] * split
            for p in range(n):
                d = jnp.dot(xs[p], w_ref[off:off + widths[p], :ncols], preferred_element_type=F32)
                acc = d if acc is None else acc + d
                off += widths[p]
            if has_res:
                acc = acc + r_ref[rows, :]
            if split == 1:
                o_ref[rows, :] = acc.astype(o_ref.dtype)
            else:
                wd = acc.shape[1] // split
                for c in range(split):
                    piece = acc[:, c * wd:(c + 1) * wd]
                    if c == 0 and first_scale is not None:
                        piece = piece * first_scale
                    o_ref[c, rows, :] = piece.astype(o_ref.dtype)


def _nmm(parts, gains, ws, layer, res=None, out_dtypes=(F32,), splits=(1,), ncols=None,
         first_scale=None, tm=512, sub=256):
    m = parts[0].shape[0]
    widths = tuple(p.shape[1] for p in parts)
    k = sum(widths)
    ncols = [w.shape[2] for w in ws] if ncols is None else ncols
    assert all(w.shape[1] == k for w in ws) and (res is None or len(ws) == 1)
    tm = min(tm, m)
    sub = min(sub, tm)
    assert m % tm == 0 and tm % sub == 0
    normed = tuple(g is not None for g in gains)
    row = lambda i: (i, 0)
    const = lambda i: (0, 0)
    in_specs = [pl.BlockSpec((tm, wd), row) for wd in widths]
    args = list(parts)
    for g, wd in zip(gains, widths):
        if g is not None:
            in_specs.append(pl.BlockSpec((1, wd), const))
            args.append(g.reshape(1, wd).astype(F32))
    for w in ws:
        in_specs.append(pl.BlockSpec((None,) + w.shape[1:], lambda i: (layer, 0, 0),
                                     pipeline_mode=pl.Buffered(1)))
        args.append(w)
    if res is not None:
        in_specs.append(pl.BlockSpec((tm, ncols[0]), row))
        args.append(res)
    out_specs, out_shapes = [], []
    for n, dt, split in zip(ncols, out_dtypes, splits):
        if split > 1:
            out_specs.append(pl.BlockSpec((split, tm, n // split), lambda i: (0, i, 0)))
            out_shapes.append(jax.ShapeDtypeStruct((split, m, n // split), dt))
        else:
            out_specs.append(pl.BlockSpec((tm, n), row))
            out_shapes.append(jax.ShapeDtypeStruct((m, n), dt))
    outs = pl.pallas_call(
        functools.partial(_nmm_kernel, widths=widths, normed=normed, splits=tuple(splits),
                          has_res=res is not None, sub=sub, first_scale=first_scale),
        grid=(m // tm,),
        in_specs=in_specs,
        out_specs=out_specs,
        out_shape=out_shapes,
        compiler_params=_cparams(("parallel",)),
        name="nmm",
    )(*args)
    return outs if len(outs) > 1 else outs[0]


def _na_kernel(q_ref, k_ref, v_ref, t_ref, o_ref, b_ref, *, patterns):
    g = pl.program_id(0)
    groups = pl.num_programs(0)

    def build(pattern):
        b_ref[...] = jnp.full(b_ref.shape, NEG_INF, F32)
        for i, (dr0, koff) in enumerate(pattern):
            for e in range(NA_ROWS):
                a = koff + e
                half = slice((a % 2) * GRID_W, (a % 2 + 1) * GRID_W)
                b_ref[:, i * GRID_W:(i + 1) * GRID_W, a * GRID_W:(a + 1) * GRID_W] = (
                    t_ref[:, dr0 + e, :, half])

    first, interior, last = patterns
    pl.when(g == 0)(lambda: build(first))
    pl.when(g == 1)(lambda: build(interior))
    pl.when(g == groups - 1)(lambda: build(last))

    for h in range(NA_HEADS):
        hs = slice(h * NA_HEAD_DIM, (h + 1) * NA_HEAD_DIM)
        s = _nt_dot(q_ref[:, hs], k_ref[0, :, hs]) + b_ref[h]
        m = jnp.max(s, axis=-1, keepdims=True)
        p = jnp.exp2(s - m)
        l = jnp.sum(p, axis=-1, keepdims=True)
        o = jnp.dot(p.astype(BF16), v_ref[0, :, hs], preferred_element_type=F32)
        o_ref[:, hs] = o / l


def _na_group_layout(rows):
    win0, patterns = [], []
    for g in range(rows // NA_GROUP_ROWS):
        w0 = int(np.clip(g * NA_GROUP_ROWS - NA_ROWS // 2, 0, rows - NA_WIN_ROWS))
        pat = []
        for i in range(NA_GROUP_ROWS):
            r = g * NA_GROUP_ROWS + i
            rs = int(np.clip(r - NA_ROWS // 2, 0, rows - NA_ROWS))
            pat.append((rs - r + NA_ROWS - 1, rs - w0))
        win0.append(w0)
        patterns.append(tuple(pat))
    return win0, patterns


def _na_bias_rows(rpb):
    qc = np.arange(GRID_W)
    col_start = np.clip(qc - NA_COLS // 2, 0, GRID_W - NA_COLS)
    kc = np.arange(GRID_W)
    inside = (kc[None, :] >= col_start[:, None]) & (kc[None, :] < col_start[:, None] + NA_COLS)
    period = 2 * GRID_W
    lead = rpb.shape[:-1]
    u = jnp.concatenate(
        [rpb[..., NA_COLS - 1:], jnp.zeros(lead + (period - 2 * NA_COLS + 1,), rpb.dtype),
         rpb[..., :NA_COLS - 1]], axis=-1)
    flat = jnp.tile(u, GRID_W)[..., :GRID_W * (period - 1)]
    toeplitz = flat.reshape(lead + (GRID_W, period - 1))[..., :GRID_W]
    full = jnp.where(inside, toeplitz * NA_LOG2E, NEG_INF).astype(F32)
    return jnp.concatenate([full, full], axis=-1)


def _na_attention(qkv, bias_rows):
    s = qkv.shape[1]
    rows = s // GRID_W
    assert rows % NA_GROUP_ROWS == 0 and rows >= NA_WIN_ROWS + NA_GROUP_ROWS
    groups = rows // NA_GROUP_ROWS
    tq = NA_GROUP_ROWS * GRID_W
    win = NA_WIN_ROWS * GRID_W
    _, patterns = _na_group_layout(rows)
    assert groups >= 3 and all(p == patterns[1] for p in patterns[1:-1])
    patterns = (patterns[0], patterns[1], patterns[-1])

    def kv_map(which):
        def index(g):
            r0 = jnp.clip(NA_GROUP_ROWS * g - NA_ROWS // 2, 0, rows - NA_WIN_ROWS)
            return (which, r0 * GRID_W, 0)
        return index

    kv_block = (pl.Element(1), pl.Element(win), pl.Element(NA_WIDTH))
    return pl.pallas_call(
        functools.partial(_na_kernel, patterns=patterns),
        grid=(groups,),
        in_specs=[
            pl.BlockSpec((None, tq, NA_WIDTH), lambda g: (0, g, 0)),
            pl.BlockSpec(kv_block, kv_map(1)),
            pl.BlockSpec(kv_block, kv_map(2)),
            pl.BlockSpec(bias_rows.shape, lambda g: (0, 0, 0, 0)),
        ],
        out_specs=pl.BlockSpec((tq, NA_WIDTH), lambda g: (g, 0)),
        out_shape=jax.ShapeDtypeStruct((s, NA_WIDTH), F32),
        scratch_shapes=[pltpu.VMEM((NA_HEADS, tq, win), F32)],
        compiler_params=_cparams(("arbitrary",)),
        name="na_attention",
    )(qkv, qkv, qkv, bias_rows)


def _latup_kernel(c_ref, qg_ref, kvg_ref, wq_ref, wqs_ref, wk_ref, wv_ref, tab_ref,
                  q_ref, k_ref, v_ref, *, scale):
    c = c_ref[...]
    qn = _rms(c[:, :MLA_Q_RANK], qg_ref[...]).astype(BF16)
    kvn = _rms(c[:, MLA_Q_RANK:MLA_Q_RANK + MLA_KV_RANK], kvg_ref[...]).astype(BF16)
    a = jnp.dot(qn, wq_ref[...], preferred_element_type=F32)
    b = jnp.dot(qn, wqs_ref[...], preferred_element_type=F32)
    kn = jnp.dot(kvn, wk_ref[...], preferred_element_type=F32)
    v_ref[...] = _nt_dot(wv_ref[...], kvn).astype(BF16)
    tab = tab_ref[...]
    cq, sq = tab[:, :LANES], tab[:, LANES:2 * LANES]
    ck, sk = tab[:, 2 * LANES:3 * LANES], tab[:, 3 * LANES:]
    r0 = MLA_Q_RANK + MLA_KV_RANK
    kr = (c[:, r0:r0 + LANES] * ck + c[:, r0 + LANES:r0 + 2 * LANES] * sk).astype(BF16)
    for h in range(MLA_HEADS):
        o = h * MLA_QK_PAD
        q_ref[o:o + LANES, :] = (a[:, o:o + LANES] * scale).T.astype(BF16)
        q_ref[o + LANES:o + 2 * LANES, :] = (
            a[:, o + LANES:o + 2 * LANES] * cq + b[:, h * LANES:(h + 1) * LANES] * sq).T.astype(BF16)
        k_ref[:, o:o + LANES] = kn[:, h * LANES:(h + 1) * LANES].astype(BF16)
        k_ref[:, o + LANES:o + 2 * LANES] = kr


def _latent_up(c, q_gain, kv_gain, wq, wqs, wk, wvt, tab, tm):
    s = c.shape[0]
    assert s % tm == 0
    const = lambda i: (0, 0)
    row = lambda i: (i, 0)
    return pl.pallas_call(
        functools.partial(_latup_kernel, scale=MLA_SCORE_SCALE),
        grid=(s // tm,),
        in_specs=[
            pl.BlockSpec((tm, c.shape[1]), row),
            pl.BlockSpec((1, MLA_Q_RANK), const),
            pl.BlockSpec((1, MLA_KV_RANK), const),
            pl.BlockSpec(wq.shape, const),
            pl.BlockSpec(wqs.shape, const),
            pl.BlockSpec(wk.shape, const),
            pl.BlockSpec(wvt.shape, const),
            pl.BlockSpec((tm, 4 * LANES), row),
        ],
        out_specs=[
            pl.BlockSpec((MLA_HEADS * MLA_QK_PAD, tm), lambda i: (0, i)),
            pl.BlockSpec((tm, MLA_HEADS * MLA_QK_PAD), row),
            pl.BlockSpec((None, MLA_WIDTH, tm), lambda i: (i, 0, 0)),
        ],
        out_shape=[
            jax.ShapeDtypeStruct((MLA_HEADS * MLA_QK_PAD, s), BF16),
            jax.ShapeDtypeStruct((s, MLA_HEADS * MLA_QK_PAD), BF16),
            jax.ShapeDtypeStruct((s // tm, MLA_WIDTH, tm), BF16),
        ],
        compiler_params=_cparams(("parallel",)),
        name="latent_up",
    )(c, q_gain.reshape(1, -1), kv_gain.reshape(1, -1), wq, wqs, wk, wvt, tab)


def _rope_tables(s, scale):
    half = MLA_ROPE // 2
    inv = 1.0 / (ROPE_THETA ** (jnp.arange(0, MLA_ROPE, 2, dtype=F32) / MLA_ROPE))
    ang = jnp.arange(s, dtype=F32)[:, None] * inv[None, :]
    cos, sin = jnp.cos(ang), jnp.sin(ang)
    z = jnp.zeros((s, LANES - 2 * half), F32)
    ct = jnp.concatenate([cos, cos, z], axis=1)
    st = jnp.concatenate([-sin, sin, z], axis=1)
    return jnp.concatenate([ct * scale, st * scale, ct, st], axis=1)


def _mla_kernel(q_ref, k_ref, vt_ref, o_ref, acc_ref, p_ref):
    nk, _, tk = vt_ref.shape
    tq = q_ref.shape[1]
    q = q_ref[...]

    def scores(j):
        return jnp.dot(k_ref[pl.ds(pl.multiple_of(j * tk, tk), tk), :], q,
                       preferred_element_type=F32)

    def pv(j, p):
        return jnp.dot(vt_ref[j], p.astype(BF16), preferred_element_type=F32)

    def fold8(v, op):
        parts = [v[i:i + 8] for i in range(0, v.shape[0], 8)]
        while len(parts) > 1:
            parts = [op(parts[i], parts[i + 1]) for i in range(0, len(parts), 2)]
        return parts[0]

    def stream_softmax(s, m_row, p_dst):
        m_blk = jnp.broadcast_to(m_row, (MLA_ROW_BLOCK, s.shape[1]))
        csum = None
        for r in range(0, tk, MLA_ROW_BLOCK):
            e = jnp.exp2(s[r:r + MLA_ROW_BLOCK] - m_blk)
            es = fold8(e, jnp.add)
            csum = es if csum is None else csum + es
            p_dst[r:r + MLA_ROW_BLOCK, :] = e.astype(BF16)
        return jnp.sum(csum, axis=0, keepdims=True)

    panels = [slice(c, c + MLA_PANEL) for c in range(0, tq, MLA_PANEL)]
    items = [(j, c) for j in range(nk) for c in range(len(panels))]

    def panel_scores(item):
        j, c = item
        return jnp.dot(k_ref[j * tk:(j + 1) * tk, :], q_ref[:, panels[c]],
                       preferred_element_type=F32)

    m, l = {}, {}
    for w, (j, c) in enumerate(items):
        cs = panels[c]
        s = panel_scores((j, c))
        if j == 0:
            m[c] = jnp.max(fold8(s[:MLA_ROW_BLOCK], jnp.maximum), axis=0, keepdims=True)
        p_dst = p_ref.at[w % 2, :, :]
        chunk_sum = stream_softmax(s, m[c], p_dst)
        part = jnp.dot(vt_ref[j], p_dst[...], preferred_element_type=F32)
        if j == 0:
            l[c] = chunk_sum
            acc_ref[:, cs] = part
        else:
            l[c] = l[c] + chunk_sum
            acc_ref[:, cs] = acc_ref[:, cs] + part
    size = jnp.sum(jnp.abs(acc_ref[...]))
    for c, cs in enumerate(panels):
        o_ref[cs, :] = (acc_ref[:, cs] / l[c]).T
        size = size + jnp.sum(l[c])

    @pl.when(jnp.logical_not(size < MLA_MAX_FAST_SUM))
    def _():
        acc_ref[...] = jnp.zeros(acc_ref.shape, F32)

        def body(j, carry):
            m_prev, l_prev = carry
            s = scores(j)
            m_new = jnp.maximum(m_prev, jnp.max(s, axis=0, keepdims=True))
            alpha = jnp.exp2(m_prev - m_new)
            p = jnp.exp2(s - m_new)
            acc_ref[...] = alpha * acc_ref[...] + pv(j, p)
            return m_new, alpha * l_prev + jnp.sum(p, axis=0, keepdims=True)

        init = (jnp.full((1, tq), -jnp.inf, F32), jnp.zeros((1, tq), F32))
        _, l_exact = lax.fori_loop(0, nk, body, init)
        o_ref[...] = (acc_ref[...] / l_exact).T


def _mla_attention(q, k, vt, tq=2048):
    s = q.shape[1]
    nk, _, tk = vt.shape
    tq = min(tq, s)
    return pl.pallas_call(
        _mla_kernel,
        grid=(MLA_HEADS, s // tq),
        in_specs=[
            pl.BlockSpec((MLA_QK_PAD, tq), lambda h, i: (h, i)),
            pl.BlockSpec((s, MLA_QK_PAD), lambda h, i: (0, h)),
            pl.BlockSpec((nk, MLA_V, tk), lambda h, i: (0, h, 0)),
        ],
        out_specs=pl.BlockSpec((tq, MLA_V), lambda h, i: (i, h)),
        out_shape=jax.ShapeDtypeStruct((s, MLA_WIDTH), F32),
        scratch_shapes=[pltpu.VMEM((MLA_V, tq), F32), pltpu.VMEM((2, tk, min(MLA_PANEL, tq)), BF16)],
        compiler_params=_cparams(("parallel", "parallel")),
        name="mla_attention",
    )(q, k, vt)


def _xattn_kernel(q_ref, k_ref, v_ref, o_ref, *, d_model, scale):
    dh = d_model // X_HEADS
    for h in range(X_HEADS):
        q = q_ref[:, h * dh:(h + 1) * dh]
        k = k_ref[:, h * dh:(h + 1) * dh]
        v = v_ref[:, h * dh:(h + 1) * dh]
        s = _nt_dot(q, k) * scale
        m = jnp.max(s, axis=-1, keepdims=True)
        p = jnp.exp(s - m)
        l = jnp.sum(p, axis=-1, keepdims=True)
        o = jnp.dot(p.astype(BF16), v, preferred_element_type=F32) / l
        o_ref[:, h * dh:(h + 1) * dh] = o.astype(o_ref.dtype)


def _xattn(q, k, v, tq=512):
    s, d_model = q.shape
    tq = min(tq, s)
    return pl.pallas_call(
        functools.partial(_xattn_kernel, d_model=d_model, scale=(d_model // X_HEADS) ** -0.5),
        grid=(s // tq,),
        in_specs=[
            pl.BlockSpec((tq, d_model), lambda i: (i, 0)),
            pl.BlockSpec(k.shape, lambda i: (0, 0)),
            pl.BlockSpec(v.shape, lambda i: (0, 0)),
        ],
        out_specs=pl.BlockSpec((tq, d_model), lambda i: (i, 0)),
        out_shape=jax.ShapeDtypeStruct((s, d_model), BF16),
        compiler_params=_cparams(("parallel",)),
        name="mem_attention",
    )(q, k, v)


def _ffn_kernel(x_ref, g_ref, wg_ref, wu_ref, wd_ref, *rest, out_norm):
    og_ref = rest[0] if out_norm else None
    o_ref, xn_ref = rest[-2:]
    f = pl.program_id(1)

    @pl.when(f == 0)
    def _():
        x = x_ref[...]
        xn_ref[...] = _rms(x, g_ref[...]).astype(BF16)
        o_ref[...] = x

    xn = xn_ref[...]
    gate = jnp.dot(xn, wg_ref[...], preferred_element_type=F32)
    up = jnp.dot(xn, wu_ref[...], preferred_element_type=F32)
    act = (gate * jax.nn.sigmoid(gate) * up).astype(BF16)
    o_ref[...] += jnp.dot(act, wd_ref[...], preferred_element_type=F32)

    if out_norm:
        @pl.when(f == pl.num_programs(1) - 1)
        def _():
            o_ref[...] = _rms(o_ref[...], og_ref[...])


def _ffn(x, gain, wg, wu, wd, layer, out_gain=None, tm=1024, tf=512):
    s, d = x.shape
    f = wg.shape[2]
    tm = min(tm, s)
    assert s % tm == 0 and f % tf == 0
    in_specs = [
        pl.BlockSpec((tm, d), lambda i, j: (i, 0)),
        pl.BlockSpec((1, d), lambda i, j: (0, 0)),
        pl.BlockSpec((None, d, tf), lambda i, j: (layer, 0, j)),
        pl.BlockSpec((None, d, tf), lambda i, j: (layer, 0, j)),
        pl.BlockSpec((None, tf, d), lambda i, j: (layer, j, 0)),
    ]
    args = [x, gain.reshape(1, d), wg, wu, wd]
    if out_gain is not None:
        in_specs.append(pl.BlockSpec((1, d), lambda i, j: (0, 0)))
        args.append(out_gain.reshape(1, d))
    return pl.pallas_call(
        functools.partial(_ffn_kernel, out_norm=out_gain is not None),
        grid=(s // tm, f // tf),
        in_specs=in_specs,
        out_specs=pl.BlockSpec((tm, d), lambda i, j: (i, 0)),
        out_shape=jax.ShapeDtypeStruct((s, d), F32),
        scratch_shapes=[pltpu.VMEM((tm, d), BF16)],
        compiler_params=_cparams(("parallel", "arbitrary")),
        name="ffn",
    )(*args)


def _latent_w_in(w_in):
    o3 = 3 * NA_WIDTH
    o5 = o3 + MLA_Q_RANK + MLA_KV_RANK
    half = MLA_ROPE // 2
    z = jnp.zeros(w_in.shape[:2] + (LANES - MLA_ROPE,), w_in.dtype)
    w_lat = jnp.concatenate(
        [w_in[..., o3:], z, w_in[..., o5 + half:], w_in[..., o5:o5 + half], z], axis=-1)
    return w_lat.astype(BF16)


def _split_w_uq(w_uq):
    lead = w_uq.shape[:2]
    half = MLA_ROPE // 2
    w = w_uq.reshape(lead + (MLA_HEADS, MLA_QK))
    wp = jnp.pad(w, ((0, 0), (0, 0), (0, 0), (0, MLA_QK_PAD - MLA_QK)))
    ws = jnp.concatenate(
        [w[..., MLA_NOPE + half:], w[..., MLA_NOPE:MLA_NOPE + half],
         jnp.zeros(lead + (MLA_HEADS, LANES - MLA_ROPE), w.dtype)], axis=-1)
    return (wp.reshape(lead + (MLA_HEADS * MLA_QK_PAD,)).astype(BF16),
            ws.reshape(lead + (MLA_HEADS * LANES,)).astype(BF16))


def _split_w_ukv(w_ukv):
    lead = w_ukv.shape[:2]
    w = w_ukv.reshape(lead + (MLA_HEADS, MLA_NOPE + MLA_V))
    wk = w[..., :MLA_NOPE].reshape(lead + (MLA_HEADS * MLA_NOPE,))
    wv = w[..., MLA_NOPE:].reshape(lead + (MLA_HEADS * MLA_V,))
    return wk.astype(BF16), jnp.swapaxes(wv, 1, 2).astype(BF16)


def kernel(x, mem, ln_mix, w_in, q_norm, kv_norm, w_uq, w_ukv, na_rpb, na_out_norm, mla_out_norm,
           w_out, ln_mem, mem_norm, w_xq, w_xk, w_xv, w_xo, ln_ffn, w_gate, w_up, w_down,
           final_norm):
    b, s, d = x.shape
    assert b == 1
    depth = w_in.shape[0]
    x = x.reshape(s, d)
    mem2 = mem.reshape(mem.shape[1], d)
    tab = _rope_tables(s, MLA_SCORE_SCALE)
    w_in_b, w_lat = w_in.astype(BF16), _latent_w_in(w_in)
    wq, wqs = _split_w_uq(w_uq)
    wk, wvt = _split_w_ukv(w_ukv)
    bias_rows = _na_bias_rows(na_rpb)
    w_out_b, w_xq_b, w_xk_b, w_xv_b, w_xo_b = (
        w.astype(BF16) for w in (w_out, w_xq, w_xk, w_xv, w_xo))
    w_gate_b, w_up_b, w_down_b = (w.astype(BF16) for w in (w_gate, w_up, w_down))
    for l in range(depth):
        qkv, c = _nmm([x], [ln_mix[l]], [w_in_b, w_lat], l, out_dtypes=(BF16, F32),
                      splits=(3, 1), ncols=[3 * NA_WIDTH, w_lat.shape[-1]],
                      first_scale=NA_SCORE_SCALE)
        y_na = _na_attention(qkv, bias_rows[l])
        ql, kl, vl = _latent_up(c, q_norm[l], kv_norm[l], wq[l], wqs[l], wk[l], wvt[l], tab,
                                tm=min(MLA_KV_CHUNK, s))
        y_mla = _mla_attention(ql, kl, vl)
        x = _nmm([y_na, y_mla], [na_out_norm[l], mla_out_norm[l]], [w_out_b], l, res=x)
        qx = _nmm([x], [ln_mem[l]], [w_xq_b], l, out_dtypes=(BF16,))
        kx, vx = _nmm([mem2], [mem_norm[l]], [w_xk_b, w_xv_b], l, out_dtypes=(BF16, BF16),
                      splits=(1, 1))
        ox = _xattn(qx, kx, vx)
        x = _nmm([ox], [None], [w_xo_b], l, res=x)
        x = _ffn(x, ln_ffn[l], w_gate_b, w_up_b, w_down_b, l,
                 out_gain=final_norm if l == depth - 1 else None)
    return x.reshape(b, s, d)
```
